```python
import math
import jax, jax.numpy as jnp
from jax import lax
import numpy as np

D_MODEL = 1024
BATCH = 8
SEQ = 4096
DEPTH = 1
DEC_BATCH = 1
DEC_SEQ = 16384
PAST_LEN = 128

N_HEADS = 8
HEAD_DIM = 64
D_ATTN = N_HEADS * HEAD_DIM
D_CONV = D_MODEL - D_ATTN
ROT_DIM = HEAD_DIM // 4
ROPE_THETA = 500000.0
CONV_WIDTH = 31
DILATED_PATTERNS = ((128, 1), (512, 4), (2048, 16))
ATTN_BLOCK = 64
NORM_EPS = 1e-6
LN_EPS = 1e-5
D_IN = 3 * D_ATTN + D_ATTN + 2 * D_CONV + D_CONV
NEG_INF = -1e30

kernel_name = "hymba_conformer_dilated_encoder"


def rms_norm(x, g):
    xf = x.astype(jnp.float32)
    y = xf * lax.rsqrt(jnp.mean(xf * xf, axis=-1, keepdims=True) + NORM_EPS)
    return (y * g.astype(jnp.float32)).astype(x.dtype)


def layer_norm(x, g, b):
    xf = x.astype(jnp.float32)
    mu = jnp.mean(xf, axis=-1, keepdims=True)
    xc = xf - mu
    var = jnp.mean(xc * xc, axis=-1, keepdims=True)
    y = xc * lax.rsqrt(var + LN_EPS) * g.astype(jnp.float32) + b.astype(jnp.float32)
    return y.astype(x.dtype)


def rope_partial(x):
    S = x.shape[1]
    half = ROT_DIM // 2
    inv = ROPE_THETA ** (-jnp.arange(half, dtype=jnp.float32) * 2.0 / ROT_DIM)
    ang = jnp.arange(S, dtype=jnp.float32)[:, None] * inv[None, :]
    cos = jnp.cos(ang)[:, None, :]
    sin = jnp.sin(ang)[:, None, :]
    xr = x[..., :ROT_DIM].astype(jnp.float32)
    x1, x2 = xr[..., :half], xr[..., half:]
    rot = jnp.concatenate([x1 * cos - x2 * sin, x2 * cos + x1 * sin], axis=-1)
    return jnp.concatenate([rot.astype(x.dtype), x[..., ROT_DIM:]], axis=-1)


def dilated_window_partial(q, k, v, window, dilation):
    B, S, H, Dh = q.shape
    d = dilation
    half = window // (2 * d)
    blk = ATTN_BLOCK
    assert half <= blk
    L = S // d
    nb = -(-L // blk)
    Lp = nb * blk

    def strided(t):
        return t.reshape(B, L, d, H, Dh).transpose(0, 2, 3, 1, 4)

    qs = jnp.pad(strided(q), ((0, 0), (0, 0), (0, 0), (0, Lp - L), (0, 0)))
    pad_k = ((0, 0), (0, 0), (0, 0), (blk, Lp - L + blk), (0, 0))
    ks = jnp.pad(strided(k), pad_k).reshape(B, d, H, nb + 2, blk, Dh)
    vs = jnp.pad(strided(v), pad_k).reshape(B, d, H, nb + 2, blk, Dh)
    qb = qs.reshape(B, d, H, nb, blk, Dh)
    kwin = jnp.concatenate([ks[:, :, :, :-2], ks[:, :, :, 1:-1], ks[:, :, :, 2:]], axis=-2)
    vwin = jnp.concatenate([vs[:, :, :, :-2], vs[:, :, :, 1:-1], vs[:, :, :, 2:]], axis=-2)

    qi = jnp.arange(nb)[:, None] * blk + jnp.arange(blk)[None, :]
    kj = (jnp.arange(nb)[:, None] - 1) * blk + jnp.arange(3 * blk)[None, :]
    rel = kj[:, None, :] - qi[:, :, None]
    valid = (jnp.abs(rel) <= half) & (kj >= 0)[:, None, :] & (kj < L)[:, None, :]

    scale = 1.0 / math.sqrt(Dh)
    s = jnp.einsum('bdhnqc,bdhnkc->bdhnqk', qb.astype(jnp.float32), kwin.astype(jnp.float32)) * scale
    s = jnp.where(valid, s, NEG_INF)
    m = jnp.max(s, axis=-1)
    p = jnp.where(valid, jnp.exp(s - m[..., None]), 0.0)
    den = jnp.sum(p, axis=-1)
    num = jnp.einsum('bdhnqk,bdhnkc->bdhnqc', p, vwin.astype(jnp.float32))

    num = num.reshape(B, d, H, Lp, Dh)[:, :, :, :L].transpose(0, 3, 1, 2, 4).reshape(B, S, H, Dh)
    m = m.reshape(B, d, H, Lp)[:, :, :, :L].transpose(0, 3, 1, 2).reshape(B, S, H)
    den = den.reshape(B, d, H, Lp)[:, :, :, :L].transpose(0, 3, 1, 2).reshape(B, S, H)
    return num, m, den


def dilated_mixture_attention(q, k, v):
    parts = [dilated_window_partial(q, k, v, w, d) for (w, d) in DILATED_PATTERNS]
    m_all = jnp.max(jnp.stack([p[1] for p in parts], axis=0), axis=0)
    num = sum(p[0] * jnp.exp(p[1] - m_all)[..., None] for p in parts)
    den = sum(p[2] * jnp.exp(p[1] - m_all) for p in parts)
    return (num / den[..., None]).astype(q.dtype)


def conformer_conv(a, b, conv_w, conv_b, ln_g, ln_b, w_pw, b_pw):
    u = a * jax.nn.sigmoid(b)
    pad = (CONV_WIDTH - 1) // 2
    u = lax.conv_general_dilated(u, conv_w[:, None, :].astype(u.dtype), window_strides=(1,),
                                 padding=[(pad, pad)], dimension_numbers=('NWC', 'WIO', 'NWC'),
                                 feature_group_count=D_CONV) + conv_b
    u = layer_norm(u, ln_g, ln_b)
    u = jax.nn.silu(u)
    return u @ w_pw + b_pw


def encoder_layer(x, norm_pre, w_in, conv_w, conv_b, conv_ln_g, conv_ln_b, w_pw, b_pw, w_out, norm_post):
    B, S, _ = x.shape
    h = rms_norm(x, norm_pre)
    z = h @ w_in
    cuts = np.cumsum([D_ATTN, D_ATTN, D_ATTN, D_ATTN, D_CONV, D_CONV])[:-0 or None]
    q, k, v, g_attn, c_a, c_b, g_conv = jnp.split(z, [int(c) for c in cuts[:6]], axis=-1)
    q = rope_partial(q.reshape(B, S, N_HEADS, HEAD_DIM))
    k = rope_partial(k.reshape(B, S, N_HEADS, HEAD_DIM))
    v = v.reshape(B, S, N_HEADS, HEAD_DIM)
    attn = dilated_mixture_attention(q, k, v).reshape(B, S, D_ATTN) * jax.nn.silu(g_attn)
    conv = conformer_conv(c_a, c_b, conv_w, conv_b, conv_ln_g, conv_ln_b, w_pw, b_pw) * jax.nn.silu(g_conv)
    y = jnp.concatenate([attn, conv], axis=-1) @ w_out
    return x + rms_norm(y, norm_post)


def setup_inputs(seed: int = 0) -> dict:
    key = jax.random.key(seed)
    ks = jax.random.split(key, 12)
    f32 = jnp.float32
    x_prompt = jax.random.normal(ks[0], (BATCH, SEQ, D_MODEL), f32)
    x_sample = jax.random.normal(ks[1], (DEC_BATCH, DEC_SEQ, D_MODEL), f32)
    norm_pre = 1.0 + 0.05 * jax.random.normal(ks[2], (DEPTH, D_MODEL), f32)
    w_in = jax.random.normal(ks[3], (DEPTH, D_MODEL, D_IN), f32) * D_MODEL ** -0.5
    conv_w = jax.random.normal(ks[4], (DEPTH, CONV_WIDTH, D_CONV), f32) * CONV_WIDTH ** -0.5
    conv_b = 0.02 * jax.random.normal(ks[5], (DEPTH, D_CONV), f32)
    conv_ln_g = 1.0 + 0.05 * jax.random.normal(ks[6], (DEPTH, D_CONV), f32)
    conv_ln_b = 0.02 * jax.random.normal(ks[7], (DEPTH, D_CONV), f32)
    w_pw = jax.random.normal(ks[8], (DEPTH, D_CONV, D_CONV), f32) * D_CONV ** -0.5
    b_pw = 0.02 * jax.random.normal(ks[9], (DEPTH, D_CONV), f32)
    w_out = jax.random.normal(ks[10], (DEPTH, D_MODEL, D_MODEL), f32) * D_MODEL ** -0.5
    norm_post = 1.0 + 0.05 * jax.random.normal(ks[11], (DEPTH, D_MODEL), f32)
    return {"x_prompt": x_prompt, "x_sample": x_sample, "norm_pre": norm_pre, "w_in": w_in,
            "conv_w": conv_w, "conv_b": conv_b, "conv_ln_g": conv_ln_g, "conv_ln_b": conv_ln_b,
            "w_pw": w_pw, "b_pw": b_pw, "w_out": w_out, "norm_post": norm_post}


def reference(x_prompt, x_sample, norm_pre, w_in, conv_w, conv_b, conv_ln_g, conv_ln_b, w_pw, b_pw, w_out, norm_post):
    y_prompt = x_prompt
    y_sample = x_sample
    for l in range(DEPTH):
        args = (norm_pre[l], w_in[l], conv_w[l], conv_b[l], conv_ln_g[l], conv_ln_b[l],
                w_pw[l], b_pw[l], w_out[l], norm_post[l])
        y_prompt = encoder_layer(y_prompt, *args)
        y_sample = encoder_layer(y_sample, *args)
    return (y_prompt, y_sample)
```

```python
import functools

import numpy as np
import jax
import jax.numpy as jnp
from jax import lax
from jax.experimental import pallas as pl
from jax.experimental.pallas import tpu as pltpu

D_MODEL = 1024
N_HEADS = 8
HEAD_DIM = 64
D_ATTN = N_HEADS * HEAD_DIM
D_CONV = D_MODEL - D_ATTN
ROT_DIM = HEAD_DIM // 4
ROT_HALF = ROT_DIM // 2
ROPE_THETA = 500000.0
CONV_WIDTH = 31
CONV_PAD = (CONV_WIDTH - 1) // 2
DILATIONS = (1, 4, 16)
HALF_WINDOW = 64
NORM_EPS = 1e-6
LN_EPS = 1e-5
NEG_INF = -1e30

LANES = 128
N_SLABS = D_ATTN // LANES
Q_BLOCK = 2 * HALF_WINDOW
K_WINDOW = Q_BLOCK + 2 * HALF_WINDOW
HALO_ROWS = 16
LSE_REP = LANES // N_HEADS
VMEM_LIMIT_BYTES = 56 * 1024 * 1024

PROJ_TILE = 512
OUT_TILE = 256
ATTN_ROWS = 1024

BF16 = jnp.bfloat16
F32 = jnp.float32


def _compiler_params(n_grid):
    return pltpu.CompilerParams(dimension_semantics=("arbitrary",) * n_grid,
                                vmem_limit_bytes=VMEM_LIMIT_BYTES)


def _proj_kernel(x_ref, g_ref, w_ref, c_ref, s1_ref, s2_ref,
                 qn, kn, vn, q4, k4, v4, q16, k16, v16, ga, u, gc, slab):
    tile = x_ref.shape[1]
    x = x_ref[0]
    ms = jnp.mean(x * x, axis=-1, keepdims=True)
    h = (x * lax.rsqrt(ms + NORM_EPS) * g_ref[...]).astype(BF16)

    def proj(section):
        c0 = section * D_ATTN
        return jnp.dot(h, w_ref[:, c0:c0 + D_ATTN], preferred_element_type=F32)

    cos = c_ref[...]
    sin_up = s1_ref[...]
    sin_dn = s2_ref[...]

    def emit(z, o_nat, o4, o16, rope):
        for j in range(N_SLABS):
            cols = slice(j * LANES, (j + 1) * LANES)
            zs = z[:, cols]
            if rope:
                zs = (zs * cos + pltpu.roll(zs, ROT_HALF, 1) * sin_up
                      + pltpu.roll(zs, LANES - ROT_HALF, 1) * sin_dn)
            slab[j] = zs
            o_nat[0, :, cols] = zs.astype(BF16)
        for j in range(N_SLABS):
            cols = slice(j * LANES, (j + 1) * LANES)
            for d, o_d in ((4, o4), (16, o16)):
                for r in range(d):
                    o_d[0, r, :, cols] = slab[j, pl.ds(r, tile // d, stride=d), :].astype(BF16)

    emit(proj(0) * (HEAD_DIM ** -0.5), qn, q4, q16, True)
    emit(proj(1), kn, k4, k16, True)
    emit(proj(2), vn, v4, v16, False)
    ga[0] = jax.nn.silu(proj(3)).astype(BF16)
    u[0] = (proj(4) * jax.nn.sigmoid(proj(5))).astype(BF16)
    gc[0] = jax.nn.silu(proj(6)).astype(BF16)


def _rope_tables(seq):
    inv = ROPE_THETA ** (-jnp.arange(ROT_HALF, dtype=F32) * 2.0 / ROT_DIM)
    ang = jnp.arange(seq, dtype=F32)[:, None] * inv[None, :]
    cos, sin = jnp.cos(ang), jnp.sin(ang)
    rest = HEAD_DIM - ROT_DIM
    one = jnp.ones((seq, rest), F32)
    zero = jnp.zeros((seq, rest), F32)
    zero_h = jnp.zeros((seq, ROT_HALF), F32)
    c = jnp.concatenate([cos, cos, one], axis=1)
    s_up = jnp.concatenate([zero_h, sin, zero], axis=1)
    s_dn = jnp.concatenate([-sin, zero_h, zero], axis=1)
    rep = LANES // HEAD_DIM
    return tuple(jnp.tile(t, (1, rep)) for t in (c, s_up, s_dn))


def _project(x, norm_pre, w_in_bf16, tables):
    b, s, _ = x.shape
    tile = PROJ_TILE
    assert s % tile == 0 and tile % (16 * 16) == 0
    nat = jax.ShapeDtypeStruct((b, s, D_ATTN), BF16)
    res = lambda d: jax.ShapeDtypeStruct((b, d, s // d, D_ATTN), BF16)
    nat_spec = pl.BlockSpec((1, tile, D_ATTN), lambda bi, i: (bi, i, 0))
    res_spec = lambda d: pl.BlockSpec((1, d, tile // d, D_ATTN), lambda bi, i: (bi, 0, i, 0))
    tab_spec = pl.BlockSpec((tile, LANES), lambda bi, i: (i, 0))
    out_shape = [nat, nat, nat, res(4), res(4), res(4), res(16), res(16), res(16), nat, nat, nat]
    out_specs = [nat_spec] * 3 + [res_spec(4)] * 3 + [res_spec(16)] * 3 + [nat_spec] * 3
    return pl.pallas_call(
        _proj_kernel,
        grid=(b, s // tile),
        in_specs=[pl.BlockSpec((1, tile, D_MODEL), lambda bi, i: (bi, i, 0)),
                  pl.BlockSpec((1, D_MODEL), lambda bi, i: (0, 0)),
                  pl.BlockSpec(w_in_bf16.shape, lambda bi, i: (0, 0)),
                  tab_spec, tab_spec, tab_spec],
        out_specs=out_specs,
        out_shape=out_shape,
        scratch_shapes=[pltpu.VMEM((N_SLABS, tile, LANES), F32)],
        compiler_params=_compiler_params(2),
        name="proj",
    )(x, norm_pre.reshape(1, D_MODEL), w_in_bf16, *tables)


def _band_bias():
    a = np.arange(Q_BLOCK)[:, None]
    c = np.arange(K_WINDOW)[None, :]
    band = (c >= a) & (c <= a + 2 * HALF_WINDOW)
    variants = [band & (c >= HALF_WINDOW), band, band & (c < K_WINDOW - HALF_WINDOW)]
    return np.where(np.stack(variants), 0.0, NEG_INF).astype(np.float32)


def _attn_kernel(q_ref, kp_ref, kc_ref, kn_ref, vp_ref, vc_ref, vn_ref, bias_ref,
                 o_ref, lse_ref, kbuf, vbuf):
    rows = q_ref.shape[1]
    n_blocks = rows // Q_BLOCK
    step = pl.program_id(1)
    last_step = pl.num_programs(1) - 1

    kbuf[0:HALF_WINDOW] = kp_ref[0]
    kbuf[HALF_WINDOW:HALF_WINDOW + rows] = kc_ref[0]
    kbuf[HALF_WINDOW + rows:] = kn_ref[0]
    vbuf[0:HALF_WINDOW] = vp_ref[0]
    vbuf[HALF_WINDOW:HALF_WINDOW + rows] = vc_ref[0]
    vbuf[HALF_WINDOW + rows:] = vn_ref[0]

    lane = lax.broadcasted_iota(jnp.int32, (Q_BLOCK, LANES), 1)
    low_half = lane < HEAD_DIM
    lane_head = lane // LSE_REP

    def block(blk, carry):
        q0 = pl.multiple_of(blk * Q_BLOCK, Q_BLOCK)
        first = jnp.logical_and(step == 0, blk == 0)
        last = jnp.logical_and(step == last_step, blk == n_blocks - 1)
        variant = jnp.where(first, 0, jnp.where(last, 2, 1))
        bias = bias_ref[variant]
        lse_full = jnp.zeros((Q_BLOCK, LANES), F32)
        for j in range(N_SLABS):
            cols = slice(j * LANES, (j + 1) * LANES)
            qs = q_ref[0, pl.ds(q0, Q_BLOCK), cols]
            ks = kbuf[pl.ds(q0, K_WINDOW), cols]
            vs = vbuf[pl.ds(q0, K_WINDOW), cols]
            zero = jnp.zeros_like(qs)
            outs = []
            for half, qh in enumerate((jnp.where(low_half, qs, zero), jnp.where(low_half, zero, qs))):
                s = lax.dot_general(qh, ks, (((1,), (1,)), ((), ())),
                                    preferred_element_type=F32) + bias
                m = jnp.max(s, axis=-1, keepdims=True)
                p = jnp.exp(s - m)
                den = jnp.sum(p, axis=-1, keepdims=True)
                o = jnp.dot(p.astype(BF16), vs, preferred_element_type=F32)
                outs.append(o * (1.0 / den))
                lse = m + jnp.log(den)
                lse_full = jnp.where(lane_head == 2 * j + half, lse, lse_full)
            o_ref[0, pl.ds(q0, Q_BLOCK), cols] = jnp.where(low_half, outs[0], outs[1]).astype(BF16)
        lse_ref[0, pl.ds(q0, Q_BLOCK), :] = lse_full
        return carry

    lax.fori_loop(0, n_blocks, block, 0)


def _attention(q, k, v, bias):
    g, length, _ = q.shape
    rows = min(ATTN_ROWS, length)
    assert length % rows == 0 and rows % Q_BLOCK == 0 and rows >= 2 * Q_BLOCK
    steps = length // rows
    halo_per_step = rows // HALF_WINDOW
    n_halo = length // HALF_WINDOW
    cur_spec = pl.BlockSpec((1, rows, D_ATTN), lambda gi, i: (gi, i, 0))
    prev_spec = pl.BlockSpec((1, HALF_WINDOW, D_ATTN),
                             lambda gi, i: (gi, jnp.maximum(i * halo_per_step - 1, 0), 0))
    next_spec = pl.BlockSpec((1, HALF_WINDOW, D_ATTN),
                             lambda gi, i: (gi, jnp.minimum((i + 1) * halo_per_step, n_halo - 1), 0))
    return pl.pallas_call(
        _attn_kernel,
        grid=(g, steps),
        in_specs=[cur_spec, prev_spec, cur_spec, next_spec, prev_spec, cur_spec, next_spec,
                  pl.BlockSpec(bias.shape, lambda gi, i: (0, 0, 0))],
        out_specs=[cur_spec, pl.BlockSpec((1, rows, LANES), lambda gi, i: (gi, i, 0))],
        out_shape=[jax.ShapeDtypeStruct((g, length, D_ATTN), BF16),
                   jax.ShapeDtypeStruct((g, length, LANES), F32)],
        scratch_shapes=[pltpu.VMEM((rows + 2 * HALF_WINDOW, D_ATTN), BF16),
                        pltpu.VMEM((rows + 2 * HALF_WINDOW, D_ATTN), BF16)],
        compiler_params=_compiler_params(2),
        name="attn",
    )(q, k, k, k, v, v, v, bias)


def _head_expand():
    e = np.zeros((LANES, D_ATTN), np.float32)
    for h in range(N_HEADS):
        e[h * LSE_REP, h * HEAD_DIM:(h + 1) * HEAD_DIM] = 1.0
    return e


def _out_kernel(x_ref, o1_ref, l1_ref, o4_ref, l4_ref, o16_ref, l16_ref, ga_ref,
                up_ref, uc_ref, un_ref, gc_ref, exp_ref, cw_ref, cb_ref, lng_ref, lnb_ref,
                wpw_ref, bpw_ref, wout_ref, npost_ref, y_ref, oslab, lslab, uslab):
    tile = x_ref.shape[1]
    step = pl.program_id(1)
    last_step = pl.num_programs(1) - 1

    for p, (d, o_d, l_d) in enumerate(((4, o4_ref, l4_ref), (16, o16_ref, l16_ref))):
        for r in range(d):
            rows = pl.ds(r, tile // d, stride=d)
            lslab[p, rows, :] = l_d[0, r]
            for j in range(N_SLABS):
                oslab[p, j, rows, :] = o_d[0, r, :, j * LANES:(j + 1) * LANES].astype(F32)

    lses = (l1_ref[0], lslab[0], lslab[1])
    top = jnp.maximum(jnp.maximum(lses[0], lses[1]), lses[2])
    ws = [jnp.exp(l - top) for l in lses]
    inv = 1.0 / (ws[0] + ws[1] + ws[2])
    wide = [jnp.dot((w * inv).astype(BF16), exp_ref[...], preferred_element_type=F32) for w in ws]

    attn_parts = []
    conv_parts = []
    for j in range(N_SLABS):
        cols = slice(j * LANES, (j + 1) * LANES)
        mix = (wide[0][:, cols] * o1_ref[0, :, cols].astype(F32)
               + wide[1][:, cols] * oslab[0, j] + wide[2][:, cols] * oslab[1, j])
        attn_parts.append(mix * ga_ref[0, :, cols].astype(F32))

        prev = up_ref[0, :, cols].astype(F32)
        nxt = un_ref[0, :, cols].astype(F32)
        uslab[j, 0:HALO_ROWS] = jnp.where(step == 0, jnp.zeros_like(prev), prev)
        uslab[j, HALO_ROWS:HALO_ROWS + tile] = uc_ref[0, :, cols].astype(F32)
        uslab[j, HALO_ROWS + tile:] = jnp.where(step == last_step, jnp.zeros_like(nxt), nxt)
        acc = jnp.zeros((tile, LANES), F32) + cb_ref[:, cols]
        for t in range(CONV_WIDTH):
            acc = acc + uslab[j, pl.ds(HALO_ROWS - CONV_PAD + t, tile), :] * cw_ref[t:t + 1, cols]
        conv_parts.append(acc)

    conv = jnp.concatenate(conv_parts, axis=1)
    mu = jnp.mean(conv, axis=-1, keepdims=True)
    cen = conv - mu
    var = jnp.mean(cen * cen, axis=-1, keepdims=True)
    ln = cen * lax.rsqrt(var + LN_EPS) * lng_ref[...] + lnb_ref[...]
    act = jax.nn.silu(ln).astype(BF16)
    pw = jnp.dot(act, wpw_ref[...], preferred_element_type=F32) + bpw_ref[...]
    conv_branch = (pw * gc_ref[0].astype(F32)).astype(BF16)
    attn_branch = jnp.concatenate(attn_parts, axis=1).astype(BF16)

    y = (jnp.dot(attn_branch, wout_ref[0:D_ATTN, :], preferred_element_type=F32)
         + jnp.dot(conv_branch, wout_ref[D_ATTN:, :], preferred_element_type=F32))
    ms = jnp.mean(y * y, axis=-1, keepdims=True)
    y_ref[0] = x_ref[0] + y * lax.rsqrt(ms + NORM_EPS) * npost_ref[...]


def _finish(x, o, lse, ga, u, gc, consts):
    b, s, _ = x.shape
    tile = OUT_TILE
    assert s % tile == 0 and tile % (16 * 16) == 0
    n_halo = s // HALO_ROWS
    halo_per_tile = tile // HALO_ROWS
    const2 = lambda a: pl.BlockSpec(a.shape, lambda bi, i: (0, 0))
    nat = lambda w: pl.BlockSpec((1, tile, w), lambda bi, i: (bi, i, 0))
    res = lambda d, w: pl.BlockSpec((1, d, tile // d, w), lambda bi, i: (bi, 0, i, 0))
    halo_prev = pl.BlockSpec((1, HALO_ROWS, D_CONV),
                             lambda bi, i: (bi, jnp.maximum(i * halo_per_tile - 1, 0), 0))
    halo_next = pl.BlockSpec((1, HALO_ROWS, D_CONV),
                             lambda bi, i: (bi, jnp.minimum((i + 1) * halo_per_tile, n_halo - 1), 0))
    o4 = o[1].reshape(b, 4, s // 4, D_ATTN)
    l4 = lse[1].reshape(b, 4, s // 4, LANES)
    o16 = o[2].reshape(b, 16, s // 16, D_ATTN)
    l16 = lse[2].reshape(b, 16, s // 16, LANES)
    return pl.pallas_call(
        _out_kernel,
        grid=(b, s // tile),
        in_specs=[nat(D_MODEL), nat(D_ATTN), nat(LANES), res(4, D_ATTN), res(4, LANES),
                  res(16, D_ATTN), res(16, LANES), nat(D_ATTN),
                  halo_prev, nat(D_CONV), halo_next, nat(D_CONV)] + [const2(c) for c in consts],
        out_specs=nat(D_MODEL),
        out_shape=jax.ShapeDtypeStruct((b, s, D_MODEL), F32),
        scratch_shapes=[pltpu.VMEM((2, N_SLABS, tile, LANES), F32),
                        pltpu.VMEM((2, tile, LANES), F32),
                        pltpu.VMEM((N_SLABS, tile + 2 * HALO_ROWS, LANES), F32)],
        compiler_params=_compiler_params(2),
        name="finish",
    )(x, o[0], lse[0], o4, l4, o16, l16, ga, u, u, u, gc, *consts)


def _encoder_layer(x, norm_pre, w_in_bf16, consts, bias):
    b, s, _ = x.shape
    tables = _rope_tables(s)
    qn, kn, vn, q4, k4, v4, q16, k16, v16, ga, u, gc = _project(x, norm_pre, w_in_bf16, tables)
    o, lse = [], []
    for d, (q, k, v) in zip(DILATIONS, ((qn, kn, vn), (q4, k4, v4), (q16, k16, v16))):
        flat = lambda t: t.reshape(b * d, s // d, D_ATTN)
        o_d, lse_d = _attention(flat(q), flat(k), flat(v), bias)
        o.append(o_d)
        lse.append(lse_d)
    return _finish(x, o, lse, ga, u, gc, consts)


def kernel(x_prompt, x_sample, norm_pre, w_in, conv_w, conv_b, conv_ln_g, conv_ln_b, w_pw, b_pw, w_out, norm_post):
    depth = norm_pre.shape[0]
    bias = jnp.asarray(_band_bias())
    expand = jnp.asarray(_head_expand(), dtype=BF16)
    y_prompt, y_sample = x_prompt, x_sample
    for l in range(depth):
        row = lambda a: a[l].reshape(1, -1)
        consts = (expand, conv_w[l], row(conv_b), row(conv_ln_g), row(conv_ln_b),
                  w_pw[l].astype(BF16), row(b_pw), w_out[l].astype(BF16), row(norm_post))
        w_in_bf16 = w_in[l].astype(BF16)
        y_prompt = _encoder_layer(y_prompt, norm_pre[l], w_in_bf16, consts, bias)
        y_sample = _encoder_layer(y_sample, norm_pre[l], w_in_bf16, consts, bias)
    return (y_prompt, y_sample)
```

```python
import functools

import numpy as np
import jax
import jax.numpy as jnp
from jax import lax
from jax.experimental import pallas as pl
from jax.experimental.pallas import tpu as pltpu

D_MODEL = 1024
N_HEADS = 8
HEAD_DIM = 64
D_ATTN = N_HEADS * HEAD_DIM
D_CONV = D_MODEL - D_ATTN
ROT_DIM = HEAD_DIM // 4
ROT_HALF = ROT_DIM // 2
ROPE_THETA = 500000.0
CONV_WIDTH = 31
CONV_PAD = (CONV_WIDTH - 1) // 2
DILATIONS = (1, 4, 16)
HALF_WINDOW = 64
NORM_EPS = 1e-6
LN_EPS = 1e-5
NEG_INF = -1e30
LOG2_E = 1.4426950408889634

LANES = 128
N_SLABS = D_ATTN // LANES
Q_BLOCK = 2 * HALF_WINDOW
K_WINDOW = Q_BLOCK + 2 * HALF_WINDOW
HALO_ROWS = 16
VMEM_LIMIT_BYTES = 56 * 1024 * 1024

PROJ_TILE = 512
OUT_TILE = 256
ATTN_ROWS = 1024

BF16 = jnp.bfloat16
F32 = jnp.float32


def _compiler_params(n_grid):
    return pltpu.CompilerParams(dimension_semantics=("arbitrary",) * n_grid,
                                vmem_limit_bytes=VMEM_LIMIT_BYTES)


def _proj_kernel(x_ref, g_ref, w_ref, c_ref, s1_ref, s2_ref,
                 qn, kn, vn, q4, k4, v4, q16, k16, v16, ga, u, gc, slab):
    tile = x_ref.shape[1]
    x = x_ref[0]
    ms = jnp.mean(x * x, axis=-1, keepdims=True)
    h = (x * lax.rsqrt(ms + NORM_EPS) * g_ref[...]).astype(BF16)

    def proj(section):
        c0 = section * D_ATTN
        return jnp.dot(h, w_ref[:, c0:c0 + D_ATTN], preferred_element_type=F32)

    cos = c_ref[...]
    sin_up = s1_ref[...]
    sin_dn = s2_ref[...]

    def emit(z, o_nat, o4, o16, rope):
        for j in range(N_SLABS):
            cols = slice(j * LANES, (j + 1) * LANES)
            zs = z[:, cols]
            if rope:
                zs = (zs * cos + pltpu.roll(zs, ROT_HALF, 1) * sin_up
                      + pltpu.roll(zs, LANES - ROT_HALF, 1) * sin_dn)
            slab[j] = zs
            o_nat[0, :, cols] = zs.astype(BF16)
        for j in range(N_SLABS):
            cols = slice(j * LANES, (j + 1) * LANES)
            for d, o_d in ((4, o4), (16, o16)):
                for r in range(d):
                    o_d[0, r, :, cols] = slab[j, pl.ds(r, tile // d, stride=d), :].astype(BF16)

    emit(proj(0) * (LOG2_E * HEAD_DIM ** -0.5), qn, q4, q16, True)
    emit(proj(1), kn, k4, k16, True)
    emit(proj(2), vn, v4, v16, False)
    ga[0] = jax.nn.silu(proj(3)).astype(BF16)
    u[0] = (proj(4) * jax.nn.sigmoid(proj(5))).astype(BF16)
    gc[0] = jax.nn.silu(proj(6)).astype(BF16)


def _rope_tables(seq):
    inv = ROPE_THETA ** (-jnp.arange(ROT_HALF, dtype=F32) * 2.0 / ROT_DIM)
    ang = jnp.arange(seq, dtype=F32)[:, None] * inv[None, :]
    cos, sin = jnp.cos(ang), jnp.sin(ang)
    rest = HEAD_DIM - ROT_DIM
    one = jnp.ones((seq, rest), F32)
    zero = jnp.zeros((seq, rest), F32)
    zero_h = jnp.zeros((seq, ROT_HALF), F32)
    c = jnp.concatenate([cos, cos, one], axis=1)
    s_up = jnp.concatenate([zero_h, sin, zero], axis=1)
    s_dn = jnp.concatenate([-sin, zero_h, zero], axis=1)
    rep = LANES // HEAD_DIM
    return tuple(jnp.tile(t, (1, rep)) for t in (c, s_up, s_dn))


def _project(x, norm_pre, w_in_bf16, tables):
    b, s, _ = x.shape
    tile = PROJ_TILE
    assert s % tile == 0 and tile % (16 * 16) == 0
    nat = jax.ShapeDtypeStruct((b, s, D_ATTN), BF16)
    res = lambda d: jax.ShapeDtypeStruct((b, d, s // d, D_ATTN), BF16)
    nat_spec = pl.BlockSpec((1, tile, D_ATTN), lambda bi, i: (bi, i, 0))
    res_spec = lambda d: pl.BlockSpec((1, d, tile // d, D_ATTN), lambda bi, i: (bi, 0, i, 0))
    tab_spec = pl.BlockSpec((tile, LANES), lambda bi, i: (i, 0))
    out_shape = [nat, nat, nat, res(4), res(4), res(4), res(16), res(16), res(16), nat, nat, nat]
    out_specs = [nat_spec] * 3 + [res_spec(4)] * 3 + [res_spec(16)] * 3 + [nat_spec] * 3
    return pl.pallas_call(
        _proj_kernel,
        grid=(b, s // tile),
        in_specs=[pl.BlockSpec((1, tile, D_MODEL), lambda bi, i: (bi, i, 0)),
                  pl.BlockSpec((1, D_MODEL), lambda bi, i: (0, 0)),
                  pl.BlockSpec(w_in_bf16.shape, lambda bi, i: (0, 0)),
                  tab_spec, tab_spec, tab_spec],
        out_specs=out_specs,
        out_shape=out_shape,
        scratch_shapes=[pltpu.VMEM((N_SLABS, tile, LANES), F32)],
        compiler_params=_compiler_params(2),
        name="proj",
    )(x, norm_pre.reshape(1, D_MODEL), w_in_bf16, *tables)


def _band_bias():
    c = np.arange(K_WINDOW)[:, None]
    a = np.arange(Q_BLOCK)[None, :]
    band = (c >= a) & (c <= a + 2 * HALF_WINDOW)
    variants = [band & (c >= HALF_WINDOW), band, band & (c < K_WINDOW - HALF_WINDOW)]
    return np.where(np.stack(variants), 0.0, NEG_INF).astype(np.float32)


def _attn_kernel(q_ref, kp_ref, kc_ref, kn_ref, vp_ref, vc_ref, vn_ref, bias_ref,
                 o_ref, lse_ref, kbuf, vtbuf):
    n_groups, rows, _ = q_ref.shape
    n_blocks = rows // Q_BLOCK
    step = pl.program_id(1)
    last_step = pl.num_programs(1) - 1

    lane = lax.broadcasted_iota(jnp.int32, (Q_BLOCK, LANES), 1)
    low_lanes = lane < HEAD_DIM
    sub = lax.broadcasted_iota(jnp.int32, (Q_BLOCK, LANES), 0)
    low_rows = sub < HEAD_DIM
    ones = jnp.ones((Q_BLOCK, LANES), BF16)

    for g in range(n_groups):
        kbuf[g, 0:HALF_WINDOW] = kp_ref[g]
        kbuf[g, HALF_WINDOW:HALF_WINDOW + rows] = kc_ref[g]
        kbuf[g, HALF_WINDOW + rows:] = kn_ref[g]

        def v_rows(r0, r1):
            if r0 < 0:
                return jnp.concatenate([vp_ref[g], vc_ref[g, 0:r1]], axis=0)
            if r1 > rows:
                return jnp.concatenate([vc_ref[g, r0:rows], vn_ref[g]], axis=0)
            return vc_ref[g, r0:r1]

        for c in range(n_blocks + 1):
            chunk = v_rows(c * Q_BLOCK - HALF_WINDOW, (c + 1) * Q_BLOCK - HALF_WINDOW)
            for j in range(N_SLABS):
                t = chunk[:, j * LANES:(j + 1) * LANES].T
                vtbuf[g, c, 0, j] = jnp.where(low_rows, t, ones)
                vtbuf[g, c, 1, j] = jnp.where(low_rows, ones, t)

    for g in range(n_groups):
        for blk in range(n_blocks):
            q0 = blk * Q_BLOCK
            if blk == 0:
                bias = bias_ref[jnp.where(step == 0, 0, 1)]
            elif blk == n_blocks - 1:
                bias = bias_ref[jnp.where(step == last_step, 2, 1)]
            else:
                bias = bias_ref[1]
            lse_rows = jnp.zeros((Q_BLOCK, LANES), F32)
            for j in range(N_SLABS):
                cols = slice(j * LANES, (j + 1) * LANES)
                qs = q_ref[g, q0:q0 + Q_BLOCK, cols]
                ks = kbuf[g, q0:q0 + K_WINDOW, cols]
                zero = jnp.zeros_like(qs)
                q_pair = jnp.concatenate([jnp.where(low_lanes, qs, zero),
                                          jnp.where(low_lanes, zero, qs)], axis=0)
                s_pair = lax.dot_general(ks, q_pair, (((1,), (1,)), ((), ())),
                                         preferred_element_type=F32)
                outs = []
                for half in range(2):
                    s = s_pair[:, half * Q_BLOCK:(half + 1) * Q_BLOCK] + bias
                    m = jnp.max(s, axis=0, keepdims=True)
                    p = jnp.exp2(s - m).astype(BF16)
                    vt = jnp.concatenate([vtbuf[g, blk, half, j], vtbuf[g, blk + 1, half, j]], axis=1)
                    o_t = jnp.dot(vt, p, preferred_element_type=F32)
                    den_row = (1 - half) * HEAD_DIM
                    den = o_t[den_row:den_row + 1, :]
                    outs.append(o_t * (1.0 / den))
                    lse_rows = jnp.where(sub == 2 * j + half, m + jnp.log2(den), lse_rows)
                o_ref[g, q0:q0 + Q_BLOCK, cols] = jnp.where(low_rows, outs[0], outs[1]).T.astype(BF16)
            lse_ref[g, q0:q0 + Q_BLOCK, :] = lse_rows.T


def _attention(q, k, v, bias):
    n, length, _ = q.shape
    rows = min(ATTN_ROWS, length)
    groups = ATTN_ROWS // rows
    assert length % rows == 0 and rows % Q_BLOCK == 0 and rows >= 2 * Q_BLOCK and n % groups == 0
    steps = length // rows
    halo_per_step = rows // HALF_WINDOW
    n_halo = length // HALF_WINDOW
    cur_spec = pl.BlockSpec((groups, rows, D_ATTN), lambda gi, i: (gi, i, 0))
    prev_spec = pl.BlockSpec((groups, HALF_WINDOW, D_ATTN),
                             lambda gi, i: (gi, jnp.maximum(i * halo_per_step - 1, 0), 0))
    next_spec = pl.BlockSpec((groups, HALF_WINDOW, D_ATTN),
                             lambda gi, i: (gi, jnp.minimum((i + 1) * halo_per_step, n_halo - 1), 0))
    n_chunks = rows // Q_BLOCK + 1
    return pl.pallas_call(
        _attn_kernel,
        grid=(n // groups, steps),
        in_specs=[cur_spec, prev_spec, cur_spec, next_spec, prev_spec, cur_spec, next_spec,
                  pl.BlockSpec(bias.shape, lambda gi, i: (0, 0, 0))],
        out_specs=[cur_spec, pl.BlockSpec((groups, rows, LANES), lambda gi, i: (gi, i, 0))],
        out_shape=[jax.ShapeDtypeStruct((n, length, D_ATTN), BF16),
                   jax.ShapeDtypeStruct((n, length, LANES), F32)],
        scratch_shapes=[pltpu.VMEM((groups, rows + 2 * HALF_WINDOW, D_ATTN), BF16),
                        pltpu.VMEM((groups, n_chunks, 2, N_SLABS, LANES, LANES), BF16)],
        compiler_params=_compiler_params(2),
        name="attn",
    )(q, k, k, k, v, v, v, bias)


def _head_expand():
    e = np.zeros((LANES, D_ATTN), np.float32)
    for h in range(N_HEADS):
        e[h, h * HEAD_DIM:(h + 1) * HEAD_DIM] = 1.0
    return e


def _out_kernel(x_ref, o1_ref, l1_ref, o4_ref, l4_ref, o16_ref, l16_ref, ga_ref,
                up_ref, uc_ref, un_ref, gc_ref, exp_ref, cw_ref, cb_ref, lng_ref, lnb_ref,
                wpw_ref, bpw_ref, wout_ref, npost_ref, y_ref, oslab, lslab, uslab):
    tile = x_ref.shape[1]
    step = pl.program_id(1)
    last_step = pl.num_programs(1) - 1

    for p, (d, o_d, l_d) in enumerate(((4, o4_ref, l4_ref), (16, o16_ref, l16_ref))):
        for r in range(d):
            rows = pl.ds(r, tile // d, stride=d)
            lslab[p, rows, :] = l_d[0, r]
            for j in range(N_SLABS):
                oslab[p, j, rows, :] = o_d[0, r, :, j * LANES:(j + 1) * LANES].astype(F32)

    lses = (l1_ref[0], lslab[0], lslab[1])
    top = jnp.maximum(jnp.maximum(lses[0], lses[1]), lses[2])
    ws = [jnp.exp2(l - top) for l in lses]
    inv = 1.0 / (ws[0] + ws[1] + ws[2])
    wide = [jnp.dot((w * inv).astype(BF16), exp_ref[...], preferred_element_type=F32) for w in ws]

    attn_parts = []
    conv_parts = []
    for j in range(N_SLABS):
        cols = slice(j * LANES, (j + 1) * LANES)
        mix = (wide[0][:, cols] * o1_ref[0, :, cols].astype(F32)
               + wide[1][:, cols] * oslab[0, j] + wide[2][:, cols] * oslab[1, j])
        attn_parts.append(mix * ga_ref[0, :, cols].astype(F32))

        prev = up_ref[0, :, cols].astype(F32)
        nxt = un_ref[0, :, cols].astype(F32)
        uslab[j, 0:HALO_ROWS] = jnp.where(step == 0, jnp.zeros_like(prev), prev)
        uslab[j, HALO_ROWS:HALO_ROWS + tile] = uc_ref[0, :, cols].astype(F32)
        uslab[j, HALO_ROWS + tile:] = jnp.where(step == last_step, jnp.zeros_like(nxt), nxt)
        acc = jnp.zeros((tile, LANES), F32) + cb_ref[:, cols]
        for t in range(CONV_WIDTH):
            acc = acc + uslab[j, pl.ds(HALO_ROWS - CONV_PAD + t, tile), :] * cw_ref[t:t + 1, cols]
        conv_parts.append(acc)

    conv = jnp.concatenate(conv_parts, axis=1)
    mu = jnp.mean(conv, axis=-1, keepdims=True)
    cen = conv - mu
    var = jnp.mean(cen * cen, axis=-1, keepdims=True)
    ln = cen * lax.rsqrt(var + LN_EPS) * lng_ref[...] + lnb_ref[...]
    act = jax.nn.silu(ln).astype(BF16)
    pw = jnp.dot(act, wpw_ref[...], preferred_element_type=F32) + bpw_ref[...]
    conv_branch = (pw * gc_ref[0].astype(F32)).astype(BF16)
    attn_branch = jnp.concatenate(attn_parts, axis=1).astype(BF16)

    y = (jnp.dot(attn_branch, wout_ref[0:D_ATTN, :], preferred_element_type=F32)
         + jnp.dot(conv_branch, wout_ref[D_ATTN:, :], preferred_element_type=F32))
    ms = jnp.mean(y * y, axis=-1, keepdims=True)
    y_ref[0] = x_ref[0] + y * lax.rsqrt(ms + NORM_EPS) * npost_ref[...]


def _finish(x, o, lse, ga, u, gc, consts):
    b, s, _ = x.shape
    tile = OUT_TILE
    assert s % tile == 0 and tile % (16 * 16) == 0
    n_halo = s // HALO_ROWS
    halo_per_tile = tile // HALO_ROWS
    const2 = lambda a: pl.BlockSpec(a.shape, lambda bi, i: (0, 0))
    nat = lambda w: pl.BlockSpec((1, tile, w), lambda bi, i: (bi, i, 0))
    res = lambda d, w: pl.BlockSpec((1, d, tile // d, w), lambda bi, i: (bi, 0, i, 0))
    halo_prev = pl.BlockSpec((1, HALO_ROWS, D_CONV),
                             lambda bi, i: (bi, jnp.maximum(i * halo_per_tile - 1, 0), 0))
    halo_next = pl.BlockSpec((1, HALO_ROWS, D_CONV),
                             lambda bi, i: (bi, jnp.minimum((i + 1) * halo_per_tile, n_halo - 1), 0))
    o4 = o[1].reshape(b, 4, s // 4, D_ATTN)
    l4 = lse[1].reshape(b, 4, s // 4, LANES)
    o16 = o[2].reshape(b, 16, s // 16, D_ATTN)
    l16 = lse[2].reshape(b, 16, s // 16, LANES)
    return pl.pallas_call(
        _out_kernel,
        grid=(b, s // tile),
        in_specs=[nat(D_MODEL), nat(D_ATTN), nat(LANES), res(4, D_ATTN), res(4, LANES),
                  res(16, D_ATTN), res(16, LANES), nat(D_ATTN),
                  halo_prev, nat(D_CONV), halo_next, nat(D_CONV)] + [const2(c) for c in consts],
        out_specs=nat(D_MODEL),
        out_shape=jax.ShapeDtypeStruct((b, s, D_MODEL), F32),
        scratch_shapes=[pltpu.VMEM((2, N_SLABS, tile, LANES), F32),
                        pltpu.VMEM((2, tile, LANES), F32),
                        pltpu.VMEM((N_SLABS, tile + 2 * HALO_ROWS, LANES), F32)],
        compiler_params=_compiler_params(2),
        name="finish",
    )(x, o[0], lse[0], o4, l4, o16, l16, ga, u, u, u, gc, *consts)


def _encoder_layer(x, norm_pre, w_in_bf16, consts, bias):
    b, s, _ = x.shape
    tables = _rope_tables(s)
    qn, kn, vn, q4, k4, v4, q16, k16, v16, ga, u, gc = _project(x, norm_pre, w_in_bf16, tables)
    o, lse = [], []
    for d, (q, k, v) in zip(DILATIONS, ((qn, kn, vn), (q4, k4, v4), (q16, k16, v16))):
        flat = lambda t: t.reshape(b * d, s // d, D_ATTN)
        o_d, lse_d = _attention(flat(q), flat(k), flat(v), bias)
        o.append(o_d)
        lse.append(lse_d)
    return _finish(x, o, lse, ga, u, gc, consts)


def kernel(x_prompt, x_sample, norm_pre, w_in, conv_w, conv_b, conv_ln_g, conv_ln_b, w_pw, b_pw, w_out, norm_post):
    depth = norm_pre.shape[0]
    bias = jnp.asarray(_band_bias())
    expand = jnp.asarray(_head_expand(), dtype=BF16)
    y_prompt, y_sample = x_prompt, x_sample
    for l in range(depth):
        row = lambda a: a[l].reshape(1, -1)
        consts = (expand, conv_w[l], row(conv_b), row(conv_ln_g), row(conv_ln_b),
                  w_pw[l].astype(BF16), row(b_pw), w_out[l].astype(BF16), row(norm_post))
        w_in_bf16 = w_in[l].astype(BF16)
        y_prompt = _encoder_layer(y_prompt, norm_pre[l], w_in_bf16, consts, bias)
        y_sample = _encoder_layer(y_sample, norm_pre[l], w_in_bf16, consts, bias)
    return (y_prompt, y_sample)
```

```python
import numpy as np
import jax
import jax.numpy as jnp
from jax import lax
from jax.experimental import pallas as pl
from jax.experimental.pallas import tpu as pltpu

D_MODEL = 1024
N_HEADS = 8
HEAD_DIM = 64
D_ATTN = N_HEADS * HEAD_DIM
D_CONV = D_MODEL - D_ATTN
ROT_DIM = HEAD_DIM // 4
ROT_HALF = ROT_DIM // 2
ROPE_THETA = 500000.0
CONV_WIDTH = 31
CONV_PAD = (CONV_WIDTH - 1) // 2
DILATIONS = (1, 4, 16)
HALF_WINDOW = 64
NORM_EPS = 1e-6
LN_EPS = 1e-5
NEG_INF = -1e30
LOG2_E = 1.4426950408889634

LANES = 128
N_SLABS = D_ATTN // LANES
Q_BLOCK = 2 * HALF_WINDOW
K_WINDOW = Q_BLOCK + 2 * HALF_WINDOW
HALO_ROWS = 16
VMEM_LIMIT_BYTES = 56 * 1024 * 1024

PROJ_TILE = 512
OUT_TILE = 512
CONV_CHUNK = 128
ATTN_ROWS = 1024

BF16 = jnp.bfloat16
F32 = jnp.float32


def _compiler_params(n_grid):
    return pltpu.CompilerParams(dimension_semantics=("arbitrary",) * n_grid,
                                vmem_limit_bytes=VMEM_LIMIT_BYTES)


def _proj_kernel(x_ref, g_ref, w_ref, c_ref, s1_ref, s2_ref,
                 qn, kn, vn, q4, k4, v4, q16, k16, v16, ga, u, gc, slab):
    tile = x_ref.shape[1]
    x = x_ref[0]
    ms = jnp.mean(x * x, axis=-1, keepdims=True)
    h = (x * lax.rsqrt(ms + NORM_EPS) * g_ref[...]).astype(BF16)

    def proj(section):
        c0 = section * D_ATTN
        return jnp.dot(h, w_ref[:, c0:c0 + D_ATTN], preferred_element_type=F32)

    cos = c_ref[...]
    sin_up = s1_ref[...]
    sin_dn = s2_ref[...]

    def emit(z, which, o_nat, o4, o16, rope):
        for j in range(N_SLABS):
            cols = slice(j * LANES, (j + 1) * LANES)
            zs = z[:, cols]
            if rope:
                zs = (zs * cos + pltpu.roll(zs, ROT_HALF, 1) * sin_up
                      + pltpu.roll(zs, LANES - ROT_HALF, 1) * sin_dn)
            slab[which, j] = zs
            o_nat[0, :, cols] = zs.astype(BF16)
        for j in range(N_SLABS):
            cols = slice(j * LANES, (j + 1) * LANES)
            for d, o_d in ((4, o4), (16, o16)):
                for r in range(d):
                    o_d[0, r, :, cols] = slab[which, j, pl.ds(r, tile // d, stride=d), :].astype(BF16)

    emit(proj(0) * (LOG2_E * HEAD_DIM ** -0.5), 0, qn, q4, q16, True)
    emit(proj(1), 1, kn, k4, k16, True)
    emit(proj(2), 2, vn, v4, v16, False)
    ga[0] = jax.nn.silu(proj(3)).astype(BF16)
    u[0] = (proj(4) * jax.nn.sigmoid(proj(5))).astype(BF16)
    gc[0] = jax.nn.silu(proj(6)).astype(BF16)


def _rope_tables(seq):
    inv = ROPE_THETA ** (-jnp.arange(ROT_HALF, dtype=F32) * 2.0 / ROT_DIM)
    ang = jnp.arange(seq, dtype=F32)[:, None] * inv[None, :]
    cos, sin = jnp.cos(ang), jnp.sin(ang)
    rest = HEAD_DIM - ROT_DIM
    one = jnp.ones((seq, rest), F32)
    zero = jnp.zeros((seq, rest), F32)
    zero_h = jnp.zeros((seq, ROT_HALF), F32)
    c = jnp.concatenate([cos, cos, one], axis=1)
    s_up = jnp.concatenate([zero_h, sin, zero], axis=1)
    s_dn = jnp.concatenate([-sin, zero_h, zero], axis=1)
    rep = LANES // HEAD_DIM
    return tuple(jnp.tile(t, (1, rep)) for t in (c, s_up, s_dn))


def _project(x, norm_pre, w_in_bf16, tables):
    b, s, _ = x.shape
    tile = PROJ_TILE
    assert s % tile == 0 and tile % (16 * 16) == 0
    nat = jax.ShapeDtypeStruct((b, s, D_ATTN), BF16)
    res = lambda d: jax.ShapeDtypeStruct((b, d, s // d, D_ATTN), BF16)
    nat_spec = pl.BlockSpec((1, tile, D_ATTN), lambda bi, i: (bi, i, 0))
    res_spec = lambda d: pl.BlockSpec((1, d, tile // d, D_ATTN), lambda bi, i: (bi, 0, i, 0))
    tab_spec = pl.BlockSpec((tile, LANES), lambda bi, i: (i, 0))
    out_shape = [nat, nat, nat, res(4), res(4), res(4), res(16), res(16), res(16), nat, nat, nat]
    out_specs = [nat_spec] * 3 + [res_spec(4)] * 3 + [res_spec(16)] * 3 + [nat_spec] * 3
    return pl.pallas_call(
        _proj_kernel,
        grid=(b, s // tile),
        in_specs=[pl.BlockSpec((1, tile, D_MODEL), lambda bi, i: (bi, i, 0)),
                  pl.BlockSpec((1, D_MODEL), lambda bi, i: (0, 0)),
                  pl.BlockSpec(w_in_bf16.shape, lambda bi, i: (0, 0)),
                  tab_spec, tab_spec, tab_spec],
        out_specs=out_specs,
        out_shape=out_shape,
        scratch_shapes=[pltpu.VMEM((3, N_SLABS, tile, LANES), F32)],
        compiler_params=_compiler_params(2),
        name="proj",
    )(x, norm_pre.reshape(1, D_MODEL), w_in_bf16, *tables)


def _band_bias():
    c = np.arange(K_WINDOW)[:, None]
    a = np.arange(Q_BLOCK)[None, :]
    band = (c >= a) & (c <= a + 2 * HALF_WINDOW)
    variants = [band & (c >= HALF_WINDOW), band, band & (c < K_WINDOW - HALF_WINDOW)]
    return np.where(np.stack(variants), 0.0, NEG_INF).astype(np.float32)


def _attn_kernel(q_ref, kp_ref, kc_ref, kn_ref, vp_ref, vc_ref, vn_ref, bias_ref,
                 o_ref, lse_ref, kbuf, vtbuf):
    n_groups, rows, _ = q_ref.shape
    n_blocks = rows // Q_BLOCK
    step = pl.program_id(1)
    last_step = pl.num_programs(1) - 1

    lane = lax.broadcasted_iota(jnp.int32, (Q_BLOCK, LANES), 1)
    low_lanes = lane < HEAD_DIM
    sub = lax.broadcasted_iota(jnp.int32, (Q_BLOCK, LANES), 0)
    low_rows = sub < HEAD_DIM
    ones = jnp.ones((Q_BLOCK, LANES), BF16)

    for g in range(n_groups):
        kbuf[g, 0:HALF_WINDOW] = kp_ref[g]
        kbuf[g, HALF_WINDOW:HALF_WINDOW + rows] = kc_ref[g]
        kbuf[g, HALF_WINDOW + rows:] = kn_ref[g]

        def v_rows(r0, r1):
            if r0 < 0:
                return jnp.concatenate([vp_ref[g], vc_ref[g, 0:r1]], axis=0)
            if r1 > rows:
                return jnp.concatenate([vc_ref[g, r0:rows], vn_ref[g]], axis=0)
            return vc_ref[g, r0:r1]

        for c in range(n_blocks + 1):
            chunk = v_rows(c * Q_BLOCK - HALF_WINDOW, (c + 1) * Q_BLOCK - HALF_WINDOW)
            for j in range(N_SLABS):
                t = chunk[:, j * LANES:(j + 1) * LANES].T
                vtbuf[g, c, 0, j] = jnp.where(low_rows, t, ones)
                vtbuf[g, c, 1, j] = jnp.where(low_rows, ones, t)

    for g in range(n_groups):
        for blk in range(n_blocks):
            q0 = blk * Q_BLOCK
            if blk == 0:
                bias = bias_ref[jnp.where(step == 0, 0, 1)]
            elif blk == n_blocks - 1:
                bias = bias_ref[jnp.where(step == last_step, 2, 1)]
            else:
                bias = bias_ref[1]
            lse_rows = jnp.zeros((Q_BLOCK, LANES), F32)
            for j in range(N_SLABS):
                cols = slice(j * LANES, (j + 1) * LANES)
                qs = q_ref[g, q0:q0 + Q_BLOCK, cols]
                ks = kbuf[g, q0:q0 + K_WINDOW, cols]
                zero = jnp.zeros_like(qs)
                q_pair = jnp.concatenate([jnp.where(low_lanes, qs, zero),
                                          jnp.where(low_lanes, zero, qs)], axis=0)
                s_pair = lax.dot_general(ks, q_pair, (((1,), (1,)), ((), ())),
                                         preferred_element_type=F32)
                outs = []
                for half in range(2):
                    s = s_pair[:, half * Q_BLOCK:(half + 1) * Q_BLOCK] + bias
                    m = jnp.max(s, axis=0, keepdims=True)
                    p = jnp.exp2(s - m).astype(BF16)
                    vt = jnp.concatenate([vtbuf[g, blk, half, j], vtbuf[g, blk + 1, half, j]], axis=1)
                    o_t = jnp.dot(vt, p, preferred_element_type=F32)
                    den_row = (1 - half) * HEAD_DIM
                    den = o_t[den_row:den_row + 1, :]
                    outs.append(o_t * (1.0 / den))
                    lse_rows = jnp.where(sub == 2 * j + half, m + jnp.log2(den), lse_rows)
                o_ref[g, q0:q0 + Q_BLOCK, cols] = jnp.where(low_rows, outs[0], outs[1]).T.astype(BF16)
            lse_ref[g, q0:q0 + Q_BLOCK, :] = lse_rows.T


def _attention(q, k, v, bias):
    n, length, _ = q.shape
    rows = min(ATTN_ROWS, length)
    groups = ATTN_ROWS // rows
    assert length % rows == 0 and rows % Q_BLOCK == 0 and rows >= 2 * Q_BLOCK and n % groups == 0
    steps = length // rows
    halo_per_step = rows // HALF_WINDOW
    n_halo = length // HALF_WINDOW
    cur_spec = pl.BlockSpec((groups, rows, D_ATTN), lambda gi, i: (gi, i, 0))
    prev_spec = pl.BlockSpec((groups, HALF_WINDOW, D_ATTN),
                             lambda gi, i: (gi, jnp.maximum(i * halo_per_step - 1, 0), 0))
    next_spec = pl.BlockSpec((groups, HALF_WINDOW, D_ATTN),
                             lambda gi, i: (gi, jnp.minimum((i + 1) * halo_per_step, n_halo - 1), 0))
    n_chunks = rows // Q_BLOCK + 1
    return pl.pallas_call(
        _attn_kernel,
        grid=(n // groups, steps),
        in_specs=[cur_spec, prev_spec, cur_spec, next_spec, prev_spec, cur_spec, next_spec,
                  pl.BlockSpec(bias.shape, lambda gi, i: (0, 0, 0))],
        out_specs=[cur_spec, pl.BlockSpec((groups, rows, LANES), lambda gi, i: (gi, i, 0))],
        out_shape=[jax.ShapeDtypeStruct((n, length, D_ATTN), BF16),
                   jax.ShapeDtypeStruct((n, length, LANES), F32)],
        scratch_shapes=[pltpu.VMEM((groups, rows + 2 * HALF_WINDOW, D_ATTN), BF16),
                        pltpu.VMEM((groups, n_chunks, 2, N_SLABS, LANES, LANES), BF16)],
        compiler_params=_compiler_params(2),
        name="attn",
    )(q, k, k, k, v, v, v, bias)


def _head_expand():
    e = np.zeros((LANES, D_ATTN), np.float32)
    for h in range(N_HEADS):
        e[h, h * HEAD_DIM:(h + 1) * HEAD_DIM] = 1.0
    return e


def _out_kernel(x_ref, o1_ref, l1_ref, o4_ref, l4_ref, o16_ref, l16_ref, ga_ref,
                up_ref, uc_ref, un_ref, gc_ref, exp_ref, cw_ref, cb_ref, lng_ref, lnb_ref,
                wpw_ref, bpw_ref, wout_ref, npost_ref, y_ref, oslab, lslab, uslab):
    tile = x_ref.shape[1]
    step = pl.program_id(1)
    last_step = pl.num_programs(1) - 1

    for p, (d, o_d, l_d) in enumerate(((4, o4_ref, l4_ref), (16, o16_ref, l16_ref))):
        for r in range(d):
            rows = pl.ds(r, tile // d, stride=d)
            lslab[p, rows, :] = l_d[0, r]
            for j in range(N_SLABS):
                oslab[p, j, rows, :] = o_d[0, r, :, j * LANES:(j + 1) * LANES].astype(F32)

    lses = (l1_ref[0], lslab[0], lslab[1])
    top = jnp.maximum(jnp.maximum(lses[0], lses[1]), lses[2])
    ws = [jnp.exp2(l - top) for l in lses]
    inv = 1.0 / (ws[0] + ws[1] + ws[2])
    wide = [jnp.dot((w * inv).astype(BF16), exp_ref[...], preferred_element_type=F32) for w in ws]

    attn_parts = []
    conv_parts = []
    for j in range(N_SLABS):
        cols = slice(j * LANES, (j + 1) * LANES)
        mix = (wide[0][:, cols] * o1_ref[0, :, cols].astype(F32)
               + wide[1][:, cols] * oslab[0, j] + wide[2][:, cols] * oslab[1, j])
        attn_parts.append(mix * ga_ref[0, :, cols].astype(F32))

        prev = up_ref[0, :, cols].astype(F32)
        nxt = un_ref[0, :, cols].astype(F32)
        uslab[j, 0:HALO_ROWS] = jnp.where(step == 0, jnp.zeros_like(prev), prev)
        uslab[j, HALO_ROWS:HALO_ROWS + tile] = uc_ref[0, :, cols].astype(F32)
        uslab[j, HALO_ROWS + tile:] = jnp.where(step == last_step, jnp.zeros_like(nxt), nxt)
        chunks = []
        for r0 in range(0, tile, CONV_CHUNK):
            acc = jnp.zeros((CONV_CHUNK, LANES), F32) + cb_ref[:, cols]
            for t in range(CONV_WIDTH):
                acc = acc + (uslab[j, pl.ds(r0 + HALO_ROWS - CONV_PAD + t, CONV_CHUNK), :]
                             * cw_ref[t:t + 1, cols])
            chunks.append(acc)
        conv_parts.append(jnp.concatenate(chunks, axis=0))

    conv = jnp.concatenate(conv_parts, axis=1)
    mu = jnp.mean(conv, axis=-1, keepdims=True)
    cen = conv - mu
    var = jnp.mean(cen * cen, axis=-1, keepdims=True)
    ln = cen * lax.rsqrt(var + LN_EPS) * lng_ref[...] + lnb_ref[...]
    act = jax.nn.silu(ln).astype(BF16)
    pw = jnp.dot(act, wpw_ref[...], preferred_element_type=F32) + bpw_ref[...]
    conv_branch = (pw * gc_ref[0].astype(F32)).astype(BF16)
    attn_branch = jnp.concatenate(attn_parts, axis=1).astype(BF16)

    y = (jnp.dot(attn_branch, wout_ref[0:D_ATTN, :], preferred_element_type=F32)
         + jnp.dot(conv_branch, wout_ref[D_ATTN:, :], preferred_element_type=F32))
    ms = jnp.mean(y * y, axis=-1, keepdims=True)
    y_ref[0] = x_ref[0] + y * lax.rsqrt(ms + NORM_EPS) * npost_ref[...]


def _finish(x, o, lse, ga, u, gc, consts):
    b, s, _ = x.shape
    tile = OUT_TILE
    assert s % tile == 0 and tile % (16 * 16) == 0 and tile % CONV_CHUNK == 0
    n_halo = s // HALO_ROWS
    halo_per_tile = tile // HALO_ROWS
    const2 = lambda a: pl.BlockSpec(a.shape, lambda bi, i: (0, 0))
    nat = lambda w: pl.BlockSpec((1, tile, w), lambda bi, i: (bi, i, 0))
    res = lambda d, w: pl.BlockSpec((1, d, tile // d, w), lambda bi, i: (bi, 0, i, 0))
    halo_prev = pl.BlockSpec((1, HALO_ROWS, D_CONV),
                             lambda bi, i: (bi, jnp.maximum(i * halo_per_tile - 1, 0), 0))
    halo_next = pl.BlockSpec((1, HALO_ROWS, D_CONV),
                             lambda bi, i: (bi, jnp.minimum((i + 1) * halo_per_tile, n_halo - 1), 0))
    o4 = o[1].reshape(b, 4, s // 4, D_ATTN)
    l4 = lse[1].reshape(b, 4, s // 4, LANES)
    o16 = o[2].reshape(b, 16, s // 16, D_ATTN)
    l16 = lse[2].reshape(b, 16, s // 16, LANES)
    return pl.pallas_call(
        _out_kernel,
        grid=(b, s // tile),
        in_specs=[nat(D_MODEL), nat(D_ATTN), nat(LANES), res(4, D_ATTN), res(4, LANES),
                  res(16, D_ATTN), res(16, LANES), nat(D_ATTN),
                  halo_prev, nat(D_CONV), halo_next, nat(D_CONV)] + [const2(c) for c in consts],
        out_specs=nat(D_MODEL),
        out_shape=jax.ShapeDtypeStruct((b, s, D_MODEL), F32),
        scratch_shapes=[pltpu.VMEM((2, N_SLABS, tile, LANES), F32),
                        pltpu.VMEM((2, tile, LANES), F32),
                        pltpu.VMEM((N_SLABS, tile + 2 * HALO_ROWS, LANES), F32)],
        compiler_params=_compiler_params(2),
        name="finish",
    )(x, o[0], lse[0], o4, l4, o16, l16, ga, u, u, u, gc, *consts)


def _encoder_layer(x, norm_pre, w_in_bf16, consts, bias):
    b, s, _ = x.shape
    tables = _rope_tables(s)
    qn, kn, vn, q4, k4, v4, q16, k16, v16, ga, u, gc = _project(x, norm_pre, w_in_bf16, tables)
    o, lse = [], []
    for d, (q, k, v) in zip(DILATIONS, ((qn, kn, vn), (q4, k4, v4), (q16, k16, v16))):
        flat = lambda t: t.reshape(b * d, s // d, D_ATTN)
        o_d, lse_d = _attention(flat(q), flat(k), flat(v), bias)
        o.append(o_d)
        lse.append(lse_d)
    return _finish(x, o, lse, ga, u, gc, consts)


def kernel(x_prompt, x_sample, norm_pre, w_in, conv_w, conv_b, conv_ln_g, conv_ln_b, w_pw, b_pw, w_out, norm_post):
    depth = norm_pre.shape[0]
    bias = jnp.asarray(_band_bias())
    expand = jnp.asarray(_head_expand(), dtype=BF16)
    y_prompt, y_sample = x_prompt, x_sample
    for l in range(depth):
        row = lambda a: a[l].reshape(1, -1)
        consts = (expand, conv_w[l], row(conv_b), row(conv_ln_g), row(conv_ln_b),
                  w_pw[l].astype(BF16), row(b_pw), w_out[l].astype(BF16), row(norm_post))
        w_in_bf16 = w_in[l].astype(BF16)
        y_prompt = _encoder_layer(y_prompt, norm_pre[l], w_in_bf16, consts, bias)
        y_sample = _encoder_layer(y_sample, norm_pre[l], w_in_bf16, consts, bias)
    return (y_prompt, y_sample)
```

```python
import numpy as np
import jax
import jax.numpy as jnp
from jax import lax
from jax.experimental import pallas as pl
from jax.experimental.pallas import tpu as pltpu

D_MODEL = 1024
N_HEADS = 8
HEAD_DIM = 64
D_ATTN = N_HEADS * HEAD_DIM
D_CONV = D_MODEL - D_ATTN
ROT_DIM = HEAD_DIM // 4
ROT_HALF = ROT_DIM // 2
ROPE_THETA = 500000.0
CONV_WIDTH = 31
CONV_PAD = (CONV_WIDTH - 1) // 2
DILATIONS = (1, 4, 16)
HALF_WINDOW = 64
NORM_EPS = 1e-6
LN_EPS = 1e-5
NEG_INF = -1e30
LOG2_E = 1.4426950408889634

LANES = 128
N_SLABS = D_ATTN // LANES
Q_BLOCK = 2 * HALF_WINDOW
K_WINDOW = Q_BLOCK + 2 * HALF_WINDOW
HALO_ROWS = 16
VMEM_LIMIT_BYTES = 56 * 1024 * 1024

PROJ_TILE = 512
OUT_TILE = 1024
CONV_CHUNK = 128
ATTN_ROWS = 1024

BF16 = jnp.bfloat16
F32 = jnp.float32


def _compiler_params(n_grid):
    return pltpu.CompilerParams(dimension_semantics=("arbitrary",) * n_grid,
                                vmem_limit_bytes=VMEM_LIMIT_BYTES)


def _proj_kernel(x_ref, g_ref, w_ref, c_ref, s1_ref, s2_ref,
                 qn, kn, vn, q4, k4, v4, q16, k16, v16, ga, u, gc, slab):
    tile = x_ref.shape[1]
    x = x_ref[0]
    ms = jnp.mean(x * x, axis=-1, keepdims=True)
    h = (x * lax.rsqrt(ms + NORM_EPS) * g_ref[...]).astype(BF16)

    def proj(section):
        c0 = section * D_ATTN
        return jnp.dot(h, w_ref[:, c0:c0 + D_ATTN], preferred_element_type=F32)

    cos = c_ref[...]
    sin_up = s1_ref[...]
    sin_dn = s2_ref[...]

    def emit(z, which, o_nat, o4, o16, rope):
        for j in range(N_SLABS):
            cols = slice(j * LANES, (j + 1) * LANES)
            zs = z[:, cols]
            if rope:
                zs = (zs * cos + pltpu.roll(zs, ROT_HALF, 1) * sin_up
                      + pltpu.roll(zs, LANES - ROT_HALF, 1) * sin_dn)
            slab[which, j] = zs
            o_nat[0, :, cols] = zs.astype(BF16)
        for j in range(N_SLABS):
            cols = slice(j * LANES, (j + 1) * LANES)
            for d, o_d in ((4, o4), (16, o16)):
                for r in range(d):
                    o_d[0, r, :, cols] = slab[which, j, pl.ds(r, tile // d, stride=d), :].astype(BF16)

    emit(proj(0) * (LOG2_E * HEAD_DIM ** -0.5), 0, qn, q4, q16, True)
    emit(proj(1), 1, kn, k4, k16, True)
    emit(proj(2), 2, vn, v4, v16, False)
    ga[0] = jax.nn.silu(proj(3)).astype(BF16)
    u[0] = (proj(4) * jax.nn.sigmoid(proj(5))).astype(BF16)
    gc[0] = jax.nn.silu(proj(6)).astype(BF16)


def _rope_tables(seq):
    lane = lax.broadcasted_iota(jnp.int32, (seq, LANES), 1) % HEAD_DIM
    pos = lax.broadcasted_iota(jnp.int32, (seq, LANES), 0).astype(F32)
    inv = ROPE_THETA ** (-(lane % ROT_HALF).astype(F32) * 2.0 / ROT_DIM)
    ang = pos * inv
    cos, sin = jnp.cos(ang), jnp.sin(ang)
    c = jnp.where(lane < ROT_DIM, cos, 1.0)
    s_up = jnp.where((lane >= ROT_HALF) & (lane < ROT_DIM), sin, 0.0)
    s_dn = jnp.where(lane < ROT_HALF, -sin, 0.0)
    return c, s_up, s_dn


def _project(x, norm_pre, w_in_bf16, tables):
    b, s, _ = x.shape
    tile = PROJ_TILE
    assert s % tile == 0 and tile % (16 * 16) == 0
    nat = jax.ShapeDtypeStruct((b, s, D_ATTN), BF16)
    res = lambda d: jax.ShapeDtypeStruct((b, d, s // d, D_ATTN), BF16)
    nat_spec = pl.BlockSpec((1, tile, D_ATTN), lambda bi, i: (bi, i, 0))
    res_spec = lambda d: pl.BlockSpec((1, d, tile // d, D_ATTN), lambda bi, i: (bi, 0, i, 0))
    tab_spec = pl.BlockSpec((tile, LANES), lambda bi, i: (i, 0))
    out_shape = [nat, nat, nat, res(4), res(4), res(4), res(16), res(16), res(16), nat, nat, nat]
    out_specs = [nat_spec] * 3 + [res_spec(4)] * 3 + [res_spec(16)] * 3 + [nat_spec] * 3
    return pl.pallas_call(
        _proj_kernel,
        grid=(b, s // tile),
        in_specs=[pl.BlockSpec((1, tile, D_MODEL), lambda bi, i: (bi, i, 0)),
                  pl.BlockSpec((1, D_MODEL), lambda bi, i: (0, 0)),
                  pl.BlockSpec(w_in_bf16.shape, lambda bi, i: (0, 0)),
                  tab_spec, tab_spec, tab_spec],
        out_specs=out_specs,
        out_shape=out_shape,
        scratch_shapes=[pltpu.VMEM((3, N_SLABS, tile, LANES), F32)],
        compiler_params=_compiler_params(2),
        name="proj",
    )(x, norm_pre.reshape(1, D_MODEL), w_in_bf16, *tables)


def _band_bias():
    c = np.arange(K_WINDOW)[:, None]
    a = np.arange(Q_BLOCK)[None, :]
    band = (c >= a) & (c <= a + 2 * HALF_WINDOW)
    variants = [band & (c >= HALF_WINDOW), band, band & (c < K_WINDOW - HALF_WINDOW)]
    return np.where(np.stack(variants), 0.0, NEG_INF).astype(np.float32)


def _attn_kernel(q_ref, kp_ref, kc_ref, kn_ref, vp_ref, vc_ref, vn_ref, bias_ref,
                 o_ref, lse_ref, kbuf, vtbuf):
    n_groups, rows, _ = q_ref.shape
    n_blocks = rows // Q_BLOCK
    step = pl.program_id(1)
    last_step = pl.num_programs(1) - 1

    lane = lax.broadcasted_iota(jnp.int32, (Q_BLOCK, LANES), 1)
    low_lanes = lane < HEAD_DIM
    sub = lax.broadcasted_iota(jnp.int32, (Q_BLOCK, LANES), 0)
    low_rows = sub < HEAD_DIM
    ones = jnp.ones((Q_BLOCK, LANES), BF16)

    for g in range(n_groups):
        kbuf[g, 0:HALF_WINDOW] = kp_ref[g]
        kbuf[g, HALF_WINDOW:HALF_WINDOW + rows] = kc_ref[g]
        kbuf[g, HALF_WINDOW + rows:] = kn_ref[g]

        def v_rows(r0, r1):
            if r0 < 0:
                return jnp.concatenate([vp_ref[g], vc_ref[g, 0:r1]], axis=0)
            if r1 > rows:
                return jnp.concatenate([vc_ref[g, r0:rows], vn_ref[g]], axis=0)
            return vc_ref[g, r0:r1]

        for c in range(n_blocks + 1):
            chunk = v_rows(c * Q_BLOCK - HALF_WINDOW, (c + 1) * Q_BLOCK - HALF_WINDOW)
            for j in range(N_SLABS):
                t = chunk[:, j * LANES:(j + 1) * LANES].T
                vtbuf[g, c, 0, j] = jnp.where(low_rows, t, ones)
                vtbuf[g, c, 1, j] = jnp.where(low_rows, ones, t)

    for g in range(n_groups):
        for blk in range(n_blocks):
            q0 = blk * Q_BLOCK
            if blk == 0:
                bias = bias_ref[jnp.where(step == 0, 0, 1)]
            elif blk == n_blocks - 1:
                bias = bias_ref[jnp.where(step == last_step, 2, 1)]
            else:
                bias = bias_ref[1]
            lse_rows = jnp.zeros((Q_BLOCK, LANES), F32)
            for j in range(N_SLABS):
                cols = slice(j * LANES, (j + 1) * LANES)
                qs = q_ref[g, q0:q0 + Q_BLOCK, cols]
                ks = kbuf[g, q0:q0 + K_WINDOW, cols]
                zero = jnp.zeros_like(qs)
                q_pair = jnp.concatenate([jnp.where(low_lanes, qs, zero),
                                          jnp.where(low_lanes, zero, qs)], axis=0)
                s_pair = lax.dot_general(ks, q_pair, (((1,), (1,)), ((), ())),
                                         preferred_element_type=F32)
                outs = []
                for half in range(2):
                    s = s_pair[:, half * Q_BLOCK:(half + 1) * Q_BLOCK] + bias
                    m = jnp.max(s, axis=0, keepdims=True)
                    p = jnp.exp2(s - m).astype(BF16)
                    vt = jnp.concatenate([vtbuf[g, blk, half, j], vtbuf[g, blk + 1, half, j]], axis=1)
                    o_t = jnp.dot(vt, p, preferred_element_type=F32)
                    den_row = (1 - half) * HEAD_DIM
                    den = o_t[den_row:den_row + 1, :]
                    outs.append(o_t * (1.0 / den))
                    lse_rows = jnp.where(sub == 2 * j + half, m + jnp.log2(den), lse_rows)
                o_ref[g, q0:q0 + Q_BLOCK, cols] = jnp.where(low_rows, outs[0], outs[1]).T.astype(BF16)
            lse_ref[g, q0:q0 + Q_BLOCK, :] = lse_rows.T


def _attention(q, k, v, bias):
    n, length, _ = q.shape
    rows = min(ATTN_ROWS, length)
    groups = ATTN_ROWS // rows
    assert length % rows == 0 and rows % Q_BLOCK == 0 and rows >= 2 * Q_BLOCK and n % groups == 0
    steps = length // rows
    halo_per_step = rows // HALF_WINDOW
    n_halo = length // HALF_WINDOW
    cur_spec = pl.BlockSpec((groups, rows, D_ATTN), lambda gi, i: (gi, i, 0))
    prev_spec = pl.BlockSpec((groups, HALF_WINDOW, D_ATTN),
                             lambda gi, i: (gi, jnp.maximum(i * halo_per_step - 1, 0), 0))
    next_spec = pl.BlockSpec((groups, HALF_WINDOW, D_ATTN),
                             lambda gi, i: (gi, jnp.minimum((i + 1) * halo_per_step, n_halo - 1), 0))
    n_chunks = rows // Q_BLOCK + 1
    return pl.pallas_call(
        _attn_kernel,
        grid=(n // groups, steps),
        in_specs=[cur_spec, prev_spec, cur_spec, next_spec, prev_spec, cur_spec, next_spec,
                  pl.BlockSpec(bias.shape, lambda gi, i: (0, 0, 0))],
        out_specs=[cur_spec, pl.BlockSpec((groups, rows, LANES), lambda gi, i: (gi, i, 0))],
        out_shape=[jax.ShapeDtypeStruct((n, length, D_ATTN), BF16),
                   jax.ShapeDtypeStruct((n, length, LANES), F32)],
        scratch_shapes=[pltpu.VMEM((groups, rows + 2 * HALF_WINDOW, D_ATTN), BF16),
                        pltpu.VMEM((groups, n_chunks, 2, N_SLABS, LANES, LANES), BF16)],
        compiler_params=_compiler_params(2),
        name="attn",
    )(q, k, k, k, v, v, v, bias)


def _head_expand():
    e = np.zeros((LANES, D_ATTN), np.float32)
    for h in range(N_HEADS):
        e[h, h * HEAD_DIM:(h + 1) * HEAD_DIM] = 1.0
    return e


def _out_kernel(x_ref, o1_ref, l1_ref, o4_ref, l4_ref, o16_ref, l16_ref, ga_ref,
                up_ref, uc_ref, un_ref, gc_ref, exp_ref, cw_ref, cb_ref, lng_ref, lnb_ref,
                wpw_ref, bpw_ref, wout_ref, npost_ref, y_ref, oslab, lslab, uslab):
    tile = x_ref.shape[1]
    step = pl.program_id(1)
    last_step = pl.num_programs(1) - 1

    for p, (d, o_d, l_d) in enumerate(((4, o4_ref, l4_ref), (16, o16_ref, l16_ref))):
        for r in range(d):
            rows = pl.ds(r, tile // d, stride=d)
            lslab[p, rows, :] = l_d[0, r]
            for j in range(N_SLABS):
                oslab[p, j, rows, :] = o_d[0, r, :, j * LANES:(j + 1) * LANES].astype(F32)

    lses = (l1_ref[0], lslab[0], lslab[1])
    top = jnp.maximum(jnp.maximum(lses[0], lses[1]), lses[2])
    ws = [jnp.exp2(l - top) for l in lses]
    inv = 1.0 / (ws[0] + ws[1] + ws[2])
    wide = [jnp.dot((w * inv).astype(BF16), exp_ref[...], preferred_element_type=F32) for w in ws]

    attn_parts = []
    conv_parts = []
    for j in range(N_SLABS):
        cols = slice(j * LANES, (j + 1) * LANES)
        mix = (wide[0][:, cols] * o1_ref[0, :, cols].astype(F32)
               + wide[1][:, cols] * oslab[0, j] + wide[2][:, cols] * oslab[1, j])
        attn_parts.append(mix * ga_ref[0, :, cols].astype(F32))

        prev = up_ref[0, :, cols].astype(F32)
        nxt = un_ref[0, :, cols].astype(F32)
        uslab[j, 0:HALO_ROWS] = jnp.where(step == 0, jnp.zeros_like(prev), prev)
        uslab[j, HALO_ROWS:HALO_ROWS + tile] = uc_ref[0, :, cols].astype(F32)
        uslab[j, HALO_ROWS + tile:] = jnp.where(step == last_step, jnp.zeros_like(nxt), nxt)
        chunks = []
        for r0 in range(0, tile, CONV_CHUNK):
            acc = jnp.zeros((CONV_CHUNK, LANES), F32) + cb_ref[:, cols]
            for t in range(CONV_WIDTH):
                acc = acc + (uslab[j, pl.ds(r0 + HALO_ROWS - CONV_PAD + t, CONV_CHUNK), :]
                             * cw_ref[t:t + 1, cols])
            chunks.append(acc)
        conv_parts.append(jnp.concatenate(chunks, axis=0))

    conv = jnp.concatenate(conv_parts, axis=1)
    mu = jnp.mean(conv, axis=-1, keepdims=True)
    cen = conv - mu
    var = jnp.mean(cen * cen, axis=-1, keepdims=True)
    ln = cen * lax.rsqrt(var + LN_EPS) * lng_ref[...] + lnb_ref[...]
    act = jax.nn.silu(ln).astype(BF16)
    pw = jnp.dot(act, wpw_ref[...], preferred_element_type=F32) + bpw_ref[...]
    conv_branch = (pw * gc_ref[0].astype(F32)).astype(BF16)
    attn_branch = jnp.concatenate(attn_parts, axis=1).astype(BF16)

    y = (jnp.dot(attn_branch, wout_ref[0:D_ATTN, :], preferred_element_type=F32)
         + jnp.dot(conv_branch, wout_ref[D_ATTN:, :], preferred_element_type=F32))
    ms = jnp.mean(y * y, axis=-1, keepdims=True)
    y_ref[0] = x_ref[0] + y * lax.rsqrt(ms + NORM_EPS) * npost_ref[...]


def _finish(x, o, lse, ga, u, gc, consts):
    b, s, _ = x.shape
    tile = OUT_TILE
    assert s % tile == 0 and tile % (16 * 16) == 0 and tile % CONV_CHUNK == 0
    n_halo = s // HALO_ROWS
    halo_per_tile = tile // HALO_ROWS
    const2 = lambda a: pl.BlockSpec(a.shape, lambda bi, i: (0, 0))
    nat = lambda w: pl.BlockSpec((1, tile, w), lambda bi, i: (bi, i, 0))
    res = lambda d, w: pl.BlockSpec((1, d, tile // d, w), lambda bi, i: (bi, 0, i, 0))
    halo_prev = pl.BlockSpec((1, HALO_ROWS, D_CONV),
                             lambda bi, i: (bi, jnp.maximum(i * halo_per_tile - 1, 0), 0))
    halo_next = pl.BlockSpec((1, HALO_ROWS, D_CONV),
                             lambda bi, i: (bi, jnp.minimum((i + 1) * halo_per_tile, n_halo - 1), 0))
    o4 = o[1].reshape(b, 4, s // 4, D_ATTN)
    l4 = lse[1].reshape(b, 4, s // 4, LANES)
    o16 = o[2].reshape(b, 16, s // 16, D_ATTN)
    l16 = lse[2].reshape(b, 16, s // 16, LANES)
    return pl.pallas_call(
        _out_kernel,
        grid=(b, s // tile),
        in_specs=[nat(D_MODEL), nat(D_ATTN), nat(LANES), res(4, D_ATTN), res(4, LANES),
                  res(16, D_ATTN), res(16, LANES), nat(D_ATTN),
                  halo_prev, nat(D_CONV), halo_next, nat(D_CONV)] + [const2(c) for c in consts],
        out_specs=nat(D_MODEL),
        out_shape=jax.ShapeDtypeStruct((b, s, D_MODEL), F32),
        scratch_shapes=[pltpu.VMEM((2, N_SLABS, tile, LANES), F32),
                        pltpu.VMEM((2, tile, LANES), F32),
                        pltpu.VMEM((N_SLABS, tile + 2 * HALO_ROWS, LANES), F32)],
        compiler_params=_compiler_params(2),
        name="finish",
    )(x, o[0], lse[0], o4, l4, o16, l16, ga, u, u, u, gc, *consts)


def _encoder_layer(x, norm_pre, w_in_bf16, tables, consts, bias):
    b, s, _ = x.shape
    qn, kn, vn, q4, k4, v4, q16, k16, v16, ga, u, gc = _project(x, norm_pre, w_in_bf16, tables)
    o, lse = [], []
    for d, (q, k, v) in zip(DILATIONS, ((qn, kn, vn), (q4, k4, v4), (q16, k16, v16))):
        flat = lambda t: t.reshape(b * d, s // d, D_ATTN)
        o_d, lse_d = _attention(flat(q), flat(k), flat(v), bias)
        o.append(o_d)
        lse.append(lse_d)
    return _finish(x, o, lse, ga, u, gc, consts)


def kernel(x_prompt, x_sample, norm_pre, w_in, conv_w, conv_b, conv_ln_g, conv_ln_b, w_pw, b_pw, w_out, norm_post):
    depth = norm_pre.shape[0]
    bias = jnp.asarray(_band_bias())
    expand = jnp.asarray(_head_expand(), dtype=BF16)
    y_prompt, y_sample = x_prompt, x_sample
    tables = _rope_tables(max(x_prompt.shape[1], x_sample.shape[1]))
    for l in range(depth):
        row = lambda a: a[l].reshape(1, -1)
        consts = (expand, conv_w[l], row(conv_b), row(conv_ln_g), row(conv_ln_b),
                  w_pw[l].astype(BF16), row(b_pw), w_out[l].astype(BF16), row(norm_post))
        w_in_bf16 = w_in[l].astype(BF16)
        y_prompt = _encoder_layer(y_prompt, norm_pre[l], w_in_bf16, tables, consts, bias)
        y_sample = _encoder_layer(y_sample, norm_pre[l], w_in_bf16, tables, consts, bias)
    return (y_prompt, y_sample)
```

```python
import numpy as np
import jax
import jax.numpy as jnp
from jax import lax
from jax.experimental import pallas as pl
from jax.experimental.pallas import tpu as pltpu

D_MODEL = 1024
N_HEADS = 8
HEAD_DIM = 64
D_ATTN = N_HEADS * HEAD_DIM
D_CONV = D_MODEL - D_ATTN
ROT_DIM = HEAD_DIM // 4
ROT_HALF = ROT_DIM // 2
ROPE_THETA = 500000.0
CONV_WIDTH = 31
CONV_PAD = (CONV_WIDTH - 1) // 2
DILATIONS = (1, 4, 16)
HALF_WINDOW = 64
NORM_EPS = 1e-6
LN_EPS = 1e-5
NEG_INF = -1e30
LOG2_E = 1.4426950408889634

LANES = 128
N_SLABS = D_ATTN // LANES
Q_BLOCK = 2 * HALF_WINDOW
K_WINDOW = Q_BLOCK + 2 * HALF_WINDOW
HALO_ROWS = 16
VMEM_LIMIT_BYTES = 56 * 1024 * 1024

PROJ_TILE = 512
OUT_TILE = 1024
CONV_CHUNK = 128
ATTN_ROWS = 1024

BF16 = jnp.bfloat16
F32 = jnp.float32


def _compiler_params(n_grid):
    return pltpu.CompilerParams(dimension_semantics=("arbitrary",) * n_grid,
                                vmem_limit_bytes=VMEM_LIMIT_BYTES)


def _proj_kernel(x_ref, g_ref, w_ref, cb_ref, sb_ref, co_ref, so_ref,
                 qn, kn, vn, q4, k4, v4, q16, k16, v16, ga, u, gc, slab):
    tile = x_ref.shape[1]
    x = x_ref[0]
    ms = jnp.mean(x * x, axis=-1, keepdims=True)
    h = (x * lax.rsqrt(ms + NORM_EPS) * g_ref[...]).astype(BF16)

    def proj(section):
        c0 = section * D_ATTN
        return jnp.dot(h, w_ref[:, c0:c0 + D_ATTN], preferred_element_type=F32)

    cos_b, sin_b = cb_ref[0, 0:1, :], sb_ref[0, 0:1, :]
    cos_o, sin_o = co_ref[...], so_ref[...]
    cos = cos_b * cos_o - sin_b * sin_o
    sin = sin_b * cos_o + cos_b * sin_o
    lane = lax.broadcasted_iota(jnp.int32, (tile, LANES), 1) % HEAD_DIM
    sin_up = jnp.where(lane >= ROT_HALF, sin, 0.0)
    sin_dn = jnp.where(lane < ROT_HALF, -sin, 0.0)

    def emit(z, which, o_nat, o4, o16, rope):
        for j in range(N_SLABS):
            cols = slice(j * LANES, (j + 1) * LANES)
            zs = z[:, cols]
            if rope:
                zs = (zs * cos + pltpu.roll(zs, ROT_HALF, 1) * sin_up
                      + pltpu.roll(zs, LANES - ROT_HALF, 1) * sin_dn)
            slab[which, j] = zs
            o_nat[0, :, cols] = zs.astype(BF16)
        for j in range(N_SLABS):
            cols = slice(j * LANES, (j + 1) * LANES)
            for d, o_d in ((4, o4), (16, o16)):
                for r in range(d):
                    o_d[0, r, :, cols] = slab[which, j, pl.ds(r, tile // d, stride=d), :].astype(BF16)

    emit(proj(0) * (LOG2_E * HEAD_DIM ** -0.5), 0, qn, q4, q16, True)
    emit(proj(1), 1, kn, k4, k16, True)
    emit(proj(2), 2, vn, v4, v16, False)
    ga[0] = jax.nn.silu(proj(3)).astype(BF16)
    u[0] = (proj(4) * jax.nn.sigmoid(proj(5))).astype(BF16)
    gc[0] = jax.nn.silu(proj(6)).astype(BF16)


def _rope_tables(seq):
    def tables(pos_col):
        n = pos_col.shape[0]
        lane = lax.broadcasted_iota(jnp.int32, (n, LANES), 1) % HEAD_DIM
        inv = ROPE_THETA ** (-(lane % ROT_HALF).astype(F32) * 2.0 / ROT_DIM)
        ang = pos_col.astype(F32) * inv
        rotary = lane < ROT_DIM
        return jnp.where(rotary, jnp.cos(ang), 1.0), jnp.where(rotary, jnp.sin(ang), 0.0)

    n_tiles = seq // PROJ_TILE
    cos_b, sin_b = tables((jnp.arange(n_tiles) * PROJ_TILE)[:, None])
    cos_o, sin_o = tables(jnp.arange(PROJ_TILE)[:, None])
    rep = lambda t: jnp.broadcast_to(t[:, None, :], (n_tiles, 8, LANES))
    return rep(cos_b), rep(sin_b), cos_o, sin_o


def _project(x, norm_pre, w_in_bf16, tables):
    b, s, _ = x.shape
    tile = PROJ_TILE
    assert s % tile == 0 and tile % (16 * 16) == 0
    nat = jax.ShapeDtypeStruct((b, s, D_ATTN), BF16)
    res = lambda d: jax.ShapeDtypeStruct((b, d, s // d, D_ATTN), BF16)
    nat_spec = pl.BlockSpec((1, tile, D_ATTN), lambda bi, i: (bi, i, 0))
    res_spec = lambda d: pl.BlockSpec((1, d, tile // d, D_ATTN), lambda bi, i: (bi, 0, i, 0))
    base_spec = pl.BlockSpec((1, 8, LANES), lambda bi, i: (i, 0, 0))
    offs_spec = pl.BlockSpec((tile, LANES), lambda bi, i: (0, 0))
    out_shape = [nat, nat, nat, res(4), res(4), res(4), res(16), res(16), res(16), nat, nat, nat]
    out_specs = [nat_spec] * 3 + [res_spec(4)] * 3 + [res_spec(16)] * 3 + [nat_spec] * 3
    return pl.pallas_call(
        _proj_kernel,
        grid=(b, s // tile),
        in_specs=[pl.BlockSpec((1, tile, D_MODEL), lambda bi, i: (bi, i, 0)),
                  pl.BlockSpec((1, D_MODEL), lambda bi, i: (0, 0)),
                  pl.BlockSpec(w_in_bf16.shape, lambda bi, i: (0, 0)),
                  base_spec, base_spec, offs_spec, offs_spec],
        out_specs=out_specs,
        out_shape=out_shape,
        scratch_shapes=[pltpu.VMEM((3, N_SLABS, tile, LANES), F32)],
        compiler_params=_compiler_params(2),
        name="proj",
    )(x, norm_pre.reshape(1, D_MODEL), w_in_bf16, *tables)


def _band_bias():
    c = np.arange(K_WINDOW)[:, None]
    a = np.arange(Q_BLOCK)[None, :]
    band = (c >= a) & (c <= a + 2 * HALF_WINDOW)
    variants = [band & (c >= HALF_WINDOW), band, band & (c < K_WINDOW - HALF_WINDOW)]
    return np.where(np.stack(variants), 0.0, NEG_INF).astype(np.float32)


def _attn_kernel(q_ref, kp_ref, kc_ref, kn_ref, vp_ref, vc_ref, vn_ref, bias_ref,
                 o_ref, lse_ref, kbuf, vtbuf):
    n_groups, rows, _ = q_ref.shape
    n_blocks = rows // Q_BLOCK
    step = pl.program_id(1)
    last_step = pl.num_programs(1) - 1

    lane = lax.broadcasted_iota(jnp.int32, (Q_BLOCK, LANES), 1)
    low_lanes = lane < HEAD_DIM
    sub = lax.broadcasted_iota(jnp.int32, (Q_BLOCK, LANES), 0)
    low_rows = sub < HEAD_DIM
    ones = jnp.ones((Q_BLOCK, LANES), BF16)

    for g in range(n_groups):
        kbuf[g, 0:HALF_WINDOW] = kp_ref[g]
        kbuf[g, HALF_WINDOW:HALF_WINDOW + rows] = kc_ref[g]
        kbuf[g, HALF_WINDOW + rows:] = kn_ref[g]

        def v_rows(r0, r1):
            if r0 < 0:
                return jnp.concatenate([vp_ref[g], vc_ref[g, 0:r1]], axis=0)
            if r1 > rows:
                return jnp.concatenate([vc_ref[g, r0:rows], vn_ref[g]], axis=0)
            return vc_ref[g, r0:r1]

        for c in range(n_blocks + 1):
            chunk = v_rows(c * Q_BLOCK - HALF_WINDOW, (c + 1) * Q_BLOCK - HALF_WINDOW)
            for j in range(N_SLABS):
                t = chunk[:, j * LANES:(j + 1) * LANES].T
                vtbuf[g, c, 0, j] = jnp.where(low_rows, t, ones)
                vtbuf[g, c, 1, j] = jnp.where(low_rows, ones, t)

    for g in range(n_groups):
        for blk in range(n_blocks):
            q0 = blk * Q_BLOCK
            if blk == 0:
                bias = bias_ref[jnp.where(step == 0, 0, 1)]
            elif blk == n_blocks - 1:
                bias = bias_ref[jnp.where(step == last_step, 2, 1)]
            else:
                bias = bias_ref[1]
            lse_rows = jnp.zeros((Q_BLOCK, LANES), F32)
            for j in range(N_SLABS):
                cols = slice(j * LANES, (j + 1) * LANES)
                qs = q_ref[g, q0:q0 + Q_BLOCK, cols]
                ks = kbuf[g, q0:q0 + K_WINDOW, cols]
                zero = jnp.zeros_like(qs)
                q_pair = jnp.concatenate([jnp.where(low_lanes, qs, zero),
                                          jnp.where(low_lanes, zero, qs)], axis=0)
                s_pair = lax.dot_general(ks, q_pair, (((1,), (1,)), ((), ())),
                                         preferred_element_type=F32)
                outs = []
                for half in range(2):
                    s = s_pair[:, half * Q_BLOCK:(half + 1) * Q_BLOCK] + bias
                    m = jnp.max(s, axis=0, keepdims=True)
                    p = jnp.exp2(s - m).astype(BF16)
                    vt = jnp.concatenate([vtbuf[g, blk, half, j], vtbuf[g, blk + 1, half, j]], axis=1)
                    o_t = jnp.dot(vt, p, preferred_element_type=F32)
                    den_row = (1 - half) * HEAD_DIM
                    den = o_t[den_row:den_row + 1, :]
                    outs.append(o_t * (1.0 / den))
                    lse_rows = jnp.where(sub == 2 * j + half, m + jnp.log2(den), lse_rows)
                o_ref[g, q0:q0 + Q_BLOCK, cols] = jnp.where(low_rows, outs[0], outs[1]).T.astype(BF16)
            lse_ref[g, q0:q0 + Q_BLOCK, :] = lse_rows.T


def _attention(q, k, v, bias):
    n, length, _ = q.shape
    rows = min(ATTN_ROWS, length)
    groups = ATTN_ROWS // rows
    assert length % rows == 0 and rows % Q_BLOCK == 0 and rows >= 2 * Q_BLOCK and n % groups == 0
    steps = length // rows
    halo_per_step = rows // HALF_WINDOW
    n_halo = length // HALF_WINDOW
    cur_spec = pl.BlockSpec((groups, rows, D_ATTN), lambda gi, i: (gi, i, 0))
    prev_spec = pl.BlockSpec((groups, HALF_WINDOW, D_ATTN),
                             lambda gi, i: (gi, jnp.maximum(i * halo_per_step - 1, 0), 0))
    next_spec = pl.BlockSpec((groups, HALF_WINDOW, D_ATTN),
                             lambda gi, i: (gi, jnp.minimum((i + 1) * halo_per_step, n_halo - 1), 0))
    n_chunks = rows // Q_BLOCK + 1
    return pl.pallas_call(
        _attn_kernel,
        grid=(n // groups, steps),
        in_specs=[cur_spec, prev_spec, cur_spec, next_spec, prev_spec, cur_spec, next_spec,
                  pl.BlockSpec(bias.shape, lambda gi, i: (0, 0, 0))],
        out_specs=[cur_spec, pl.BlockSpec((groups, rows, LANES), lambda gi, i: (gi, i, 0))],
        out_shape=[jax.ShapeDtypeStruct((n, length, D_ATTN), BF16),
                   jax.ShapeDtypeStruct((n, length, LANES), F32)],
        scratch_shapes=[pltpu.VMEM((groups, rows + 2 * HALF_WINDOW, D_ATTN), BF16),
                        pltpu.VMEM((groups, n_chunks, 2, N_SLABS, LANES, LANES), BF16)],
        compiler_params=_compiler_params(2),
        name="attn",
    )(q, k, k, k, v, v, v, bias)


def _head_expand():
    e = np.zeros((LANES, D_ATTN), np.float32)
    for h in range(N_HEADS):
        e[h, h * HEAD_DIM:(h + 1) * HEAD_DIM] = 1.0
    return e


def _out_kernel(x_ref, o1_ref, l1_ref, o4_ref, l4_ref, o16_ref, l16_ref, ga_ref,
                up_ref, uc_ref, un_ref, gc_ref, exp_ref, cw_ref, cb_ref, lng_ref, lnb_ref,
                wpw_ref, bpw_ref, wout_ref, npost_ref, y_ref, oslab, lslab, uslab):
    tile = x_ref.shape[1]
    step = pl.program_id(1)
    last_step = pl.num_programs(1) - 1

    for p, (d, o_d, l_d) in enumerate(((4, o4_ref, l4_ref), (16, o16_ref, l16_ref))):
        for r in range(d):
            rows = pl.ds(r, tile // d, stride=d)
            lslab[p, rows, :] = l_d[0, r]
            for j in range(N_SLABS):
                oslab[p, j, rows, :] = o_d[0, r, :, j * LANES:(j + 1) * LANES].astype(F32)

    lses = (l1_ref[0], lslab[0], lslab[1])
    top = jnp.maximum(jnp.maximum(lses[0], lses[1]), lses[2])
    ws = [jnp.exp2(l - top) for l in lses]
    inv = 1.0 / (ws[0] + ws[1] + ws[2])
    wide = [jnp.dot((w * inv).astype(BF16), exp_ref[...], preferred_element_type=F32) for w in ws]

    attn_parts = []
    conv_parts = []
    for j in range(N_SLABS):
        cols = slice(j * LANES, (j + 1) * LANES)
        mix = (wide[0][:, cols] * o1_ref[0, :, cols].astype(F32)
               + wide[1][:, cols] * oslab[0, j] + wide[2][:, cols] * oslab[1, j])
        attn_parts.append(mix * ga_ref[0, :, cols].astype(F32))

        prev = up_ref[0, :, cols].astype(F32)
        nxt = un_ref[0, :, cols].astype(F32)
        uslab[j, 0:HALO_ROWS] = jnp.where(step == 0, jnp.zeros_like(prev), prev)
        uslab[j, HALO_ROWS:HALO_ROWS + tile] = uc_ref[0, :, cols].astype(F32)
        uslab[j, HALO_ROWS + tile:] = jnp.where(step == last_step, jnp.zeros_like(nxt), nxt)
        chunks = []
        for r0 in range(0, tile, CONV_CHUNK):
            acc = jnp.zeros((CONV_CHUNK, LANES), F32) + cb_ref[:, cols]
            for t in range(CONV_WIDTH):
                acc = acc + (uslab[j, pl.ds(r0 + HALO_ROWS - CONV_PAD + t, CONV_CHUNK), :]
                             * cw_ref[t:t + 1, cols])
            chunks.append(acc)
        conv_parts.append(jnp.concatenate(chunks, axis=0))

    conv = jnp.concatenate(conv_parts, axis=1)
    mu = jnp.mean(conv, axis=-1, keepdims=True)
    cen = conv - mu
    var = jnp.mean(cen * cen, axis=-1, keepdims=True)
    ln = cen * lax.rsqrt(var + LN_EPS) * lng_ref[...] + lnb_ref[...]
    act = jax.nn.silu(ln).astype(BF16)
    pw = jnp.dot(act, wpw_ref[...], preferred_element_type=F32) + bpw_ref[...]
    conv_branch = (pw * gc_ref[0].astype(F32)).astype(BF16)
    attn_branch = jnp.concatenate(attn_parts, axis=1).astype(BF16)

    y = (jnp.dot(attn_branch, wout_ref[0:D_ATTN, :], preferred_element_type=F32)
         + jnp.dot(conv_branch, wout_ref[D_ATTN:, :], preferred_element_type=F32))
    ms = jnp.mean(y * y, axis=-1, keepdims=True)
    y_ref[0] = x_ref[0] + y * lax.rsqrt(ms + NORM_EPS) * npost_ref[...]


def _finish(x, o, lse, ga, u, gc, consts):
    b, s, _ = x.shape
    tile = OUT_TILE
    assert s % tile == 0 and tile % (16 * 16) == 0 and tile % CONV_CHUNK == 0
    n_halo = s // HALO_ROWS
    halo_per_tile = tile // HALO_ROWS
    const2 = lambda a: pl.BlockSpec(a.shape, lambda bi, i: (0, 0))
    nat = lambda w: pl.BlockSpec((1, tile, w), lambda bi, i: (bi, i, 0))
    res = lambda d, w: pl.BlockSpec((1, d, tile // d, w), lambda bi, i: (bi, 0, i, 0))
    halo_prev = pl.BlockSpec((1, HALO_ROWS, D_CONV),
                             lambda bi, i: (bi, jnp.maximum(i * halo_per_tile - 1, 0), 0))
    halo_next = pl.BlockSpec((1, HALO_ROWS, D_CONV),
                             lambda bi, i: (bi, jnp.minimum((i + 1) * halo_per_tile, n_halo - 1), 0))
    o4 = o[1].reshape(b, 4, s // 4, D_ATTN)
    l4 = lse[1].reshape(b, 4, s // 4, LANES)
    o16 = o[2].reshape(b, 16, s // 16, D_ATTN)
    l16 = lse[2].reshape(b, 16, s // 16, LANES)
    return pl.pallas_call(
        _out_kernel,
        grid=(b, s // tile),
        in_specs=[nat(D_MODEL), nat(D_ATTN), nat(LANES), res(4, D_ATTN), res(4, LANES),
                  res(16, D_ATTN), res(16, LANES), nat(D_ATTN),
                  halo_prev, nat(D_CONV), halo_next, nat(D_CONV)] + [const2(c) for c in consts],
        out_specs=nat(D_MODEL),
        out_shape=jax.ShapeDtypeStruct((b, s, D_MODEL), F32),
        scratch_shapes=[pltpu.VMEM((2, N_SLABS, tile, LANES), F32),
                        pltpu.VMEM((2, tile, LANES), F32),
                        pltpu.VMEM((N_SLABS, tile + 2 * HALO_ROWS, LANES), F32)],
        compiler_params=_compiler_params(2),
        name="finish",
    )(x, o[0], lse[0], o4, l4, o16, l16, ga, u, u, u, gc, *consts)


def _encoder_layer(x, norm_pre, w_in_bf16, tables, consts, bias):
    b, s, _ = x.shape
    qn, kn, vn, q4, k4, v4, q16, k16, v16, ga, u, gc = _project(x, norm_pre, w_in_bf16, tables)
    o, lse = [], []
    for d, (q, k, v) in zip(DILATIONS, ((qn, kn, vn), (q4, k4, v4), (q16, k16, v16))):
        flat = lambda t: t.reshape(b * d, s // d, D_ATTN)
        o_d, lse_d = _attention(flat(q), flat(k), flat(v), bias)
        o.append(o_d)
        lse.append(lse_d)
    return _finish(x, o, lse, ga, u, gc, consts)


def kernel(x_prompt, x_sample, norm_pre, w_in, conv_w, conv_b, conv_ln_g, conv_ln_b, w_pw, b_pw, w_out, norm_post):
    depth = norm_pre.shape[0]
    bias = jnp.asarray(_band_bias())
    expand = jnp.asarray(_head_expand(), dtype=BF16)
    y_prompt, y_sample = x_prompt, x_sample
    tables = _rope_tables(max(x_prompt.shape[1], x_sample.shape[1]))
    for l in range(depth):
        row = lambda a: a[l].reshape(1, -1)
        consts = (expand, conv_w[l], row(conv_b), row(conv_ln_g), row(conv_ln_b),
                  w_pw[l].astype(BF16), row(b_pw), w_out[l].astype(BF16), row(norm_post))
        w_in_bf16 = w_in[l].astype(BF16)
        y_prompt = _encoder_layer(y_prompt, norm_pre[l], w_in_bf16, tables, consts, bias)
        y_sample = _encoder_layer(y_sample, norm_pre[l], w_in_bf16, tables, consts, bias)
    return (y_prompt, y_sample)
```
